```python
import math
import jax, jax.numpy as jnp
from jax import lax
import numpy as np

D_MODEL = 1024
BATCH = 8
SEQ = 4096
DEPTH = 1

CHUNK = 64
Q_BLOCK = 128
CONV_WIDTH = 31
D_CONV = D_MODEL // 2
N_HEADS = 4
HEAD_DIM = 64
V_DIM = 2 * HEAD_DIM
D_ATTN = N_HEADS * V_DIM
D_QK = N_HEADS * 2 * HEAD_DIM
N_BRANCH = 2
EPS = 1e-6
IN_GROUPS = (2 * D_CONV, D_CONV, D_QK, D_QK, D_ATTN, D_ATTN, N_BRANCH * D_MODEL)
D_IN = 2 * D_CONV + D_CONV + 2 * D_QK + 2 * D_ATTN + N_BRANCH * D_MODEL

kernel_name = "hybrid_conformer_conv_diff_attn_gated_block"


def rmsnorm(x, g):
    x32 = x.astype(jnp.float32)
    y = x32 * lax.rsqrt(jnp.mean(x32 * x32, axis=-1, keepdims=True) + EPS)
    return (y * g.astype(jnp.float32)).astype(x.dtype)


def layernorm(x, g, b):
    x32 = x.astype(jnp.float32)
    mu = jnp.mean(x32, axis=-1, keepdims=True)
    var = jnp.mean(jnp.square(x32 - mu), axis=-1, keepdims=True)
    y = (x32 - mu) * lax.rsqrt(var + EPS)
    return (y * g.astype(jnp.float32) + b.astype(jnp.float32)).astype(x.dtype)


def split_cols(proj):
    pts = np.cumsum(np.array(IN_GROUPS[:-1])).tolist()
    return jnp.split(proj, pts, axis=-1)


def conv_branch(u, z, w_dw, b_dw, ln_g, ln_b, w_proj):
    a, b = jnp.split(u, 2, axis=-1)
    h = a * jax.nn.sigmoid(b)
    hp = jnp.pad(h, ((0, 0), (CONV_WIDTH - 1, 0), (0, 0)))
    h = lax.conv_general_dilated(
        hp, w_dw[:, None, :].astype(h.dtype), window_strides=(1,), padding='VALID',
        dimension_numbers=('NWC', 'WIO', 'NWC'), feature_group_count=D_CONV) + b_dw
    h = jax.nn.silu(layernorm(h, ln_g, ln_b))
    h = h * jax.nn.silu(z)
    return h @ w_proj


def diff_attention(q, k, v, lam, lambda_init, sub_g):
    B, S = q.shape[0], q.shape[1]
    nblk = S // Q_BLOCK
    scale = 1.0 / math.sqrt(HEAD_DIM)
    slopes = 2.0 ** (-8.0 * jnp.arange(1, N_HEADS + 1, dtype=jnp.float32) / N_HEADS)
    s_pos = jnp.arange(S)
    qb = q.reshape(B, nblk, Q_BLOCK, N_HEADS, 2, HEAD_DIM).transpose(1, 0, 2, 3, 4, 5)

    def one_block(args):
        qi, i = args
        t_pos = i * Q_BLOCK + jnp.arange(Q_BLOCK)
        sc = jnp.einsum('bqhmd,bshmd->bhmqs', qi, k).astype(jnp.float32) * scale
        dist = jnp.abs(t_pos[:, None] - s_pos[None, :]).astype(jnp.float32)
        bias = -slopes[:, None, None, None] * dist[None, None]
        allowed = (s_pos[None, :] // CHUNK) <= (t_pos[:, None] // CHUNK)
        sc = jnp.where(allowed[None, None, None], sc + bias[None], -jnp.inf)
        p = jax.nn.softmax(sc, axis=-1)
        p = p[:, :, 0] - lam * p[:, :, 1]
        return jnp.einsum('bhqs,bshv->bqhv', p.astype(v.dtype), v)

    o = lax.map(one_block, (qb, jnp.arange(nblk)))
    o = o.transpose(1, 0, 2, 3, 4).reshape(B, S, N_HEADS, V_DIM)
    o = rmsnorm(o, sub_g) * (1.0 - lambda_init)
    return o.reshape(B, S, D_ATTN)


def setup_inputs(seed: int = 0) -> dict:
    key = jax.random.key(seed)
    ks = jax.random.split(key, 17)
    n = jax.random.normal
    L = DEPTH
    return {
        "x": n(ks[0], (BATCH, SEQ, D_MODEL), jnp.float32),
        "w_in": n(ks[1], (L, D_MODEL, D_IN), jnp.float32) * D_MODEL ** -0.5,
        "conv_w": n(ks[2], (L, CONV_WIDTH, D_CONV), jnp.float32) * CONV_WIDTH ** -0.5,
        "conv_b": n(ks[3], (L, D_CONV), jnp.float32) * 0.01,
        "conv_ln_g": 1.0 + 0.05 * n(ks[4], (L, D_CONV), jnp.float32),
        "conv_ln_b": 0.01 * n(ks[5], (L, D_CONV), jnp.float32),
        "w_conv_proj": n(ks[6], (L, D_CONV, D_MODEL), jnp.float32) * D_CONV ** -0.5,
        "lambda_q1": 0.1 * n(ks[7], (L, HEAD_DIM), jnp.float32),
        "lambda_k1": 0.1 * n(ks[8], (L, HEAD_DIM), jnp.float32),
        "lambda_q2": 0.1 * n(ks[9], (L, HEAD_DIM), jnp.float32),
        "lambda_k2": 0.1 * n(ks[10], (L, HEAD_DIM), jnp.float32),
        "subln_g": 1.0 + 0.05 * n(ks[11], (L, V_DIM), jnp.float32),
        "w_attn_proj": n(ks[12], (L, D_ATTN, D_MODEL), jnp.float32) * D_ATTN ** -0.5,
        "w_out": n(ks[13], (L, D_MODEL, D_MODEL), jnp.float32) * D_MODEL ** -0.5,
        "norm_pre_g": 1.0 + 0.05 * n(ks[14], (L, D_MODEL), jnp.float32),
        "norm_post_g": 1.0 + 0.05 * n(ks[15], (L, D_MODEL), jnp.float32),
    }


def reference(x, w_in, conv_w, conv_b, conv_ln_g, conv_ln_b, w_conv_proj,
              lambda_q1, lambda_k1, lambda_q2, lambda_k2, subln_g, w_attn_proj,
              w_out, norm_pre_g, norm_post_g):
    B, S, _ = x.shape
    for l in range(DEPTH):
        lambda_init = 0.8 - 0.6 * math.exp(-0.3 * l)
        h = rmsnorm(x, norm_pre_g[l])
        proj = h @ w_in[l]
        u_c, z_c, q, k, v, z_a, g = split_cols(proj)

        y_c = conv_branch(u_c, z_c, conv_w[l], conv_b[l], conv_ln_g[l], conv_ln_b[l], w_conv_proj[l])

        lam = (jnp.exp(jnp.sum(lambda_q1[l].astype(jnp.float32) * lambda_k1[l].astype(jnp.float32)))
               - jnp.exp(jnp.sum(lambda_q2[l].astype(jnp.float32) * lambda_k2[l].astype(jnp.float32)))
               + lambda_init)
        q = q.reshape(B, S, N_HEADS, 2, HEAD_DIM)
        k = k.reshape(B, S, N_HEADS, 2, HEAD_DIM)
        v = v.reshape(B, S, N_HEADS, V_DIM)
        o = diff_attention(q, k, v, lam, lambda_init, subln_g[l])
        y_a = (o * jax.nn.silu(z_a)) @ w_attn_proj[l]

        gates = jax.nn.sigmoid(g.reshape(B, S, N_BRANCH, D_MODEL))
        mixed = gates[:, :, 0] * y_c + gates[:, :, 1] * y_a
        out = mixed @ w_out[l]
        x = x + rmsnorm(out, norm_post_g[l])
    return x
```

```python
import functools
import math

import jax
import jax.numpy as jnp
from jax import lax
from jax.experimental import pallas as pl
from jax.experimental.pallas import tpu as pltpu

D_MODEL = 1024
CHUNK = 64
CONV_WIDTH = 31
D_CONV = 512
N_HEADS = 4
HEAD_DIM = 64
V_DIM = 128
D_ATTN = 512
D_QK = 512
EPS = 1e-6
LAMBDA_INIT = 0.8 - 0.6 * math.exp(-0.3 * 0)
LOG2E = math.log2(math.e)
SLOPES = tuple(2.0 ** (-8.0 * (h + 1) / N_HEADS) for h in range(N_HEADS))

BF16 = jnp.bfloat16
F32 = jnp.float32

VMEM_LIMIT_BYTES = 56 * 1024 * 1024

TM_IN = 512
TS_CONV = 256
CONV_HALO = 32
CONV_ROWS = 64
TQ = 256
TK = 256
TM_OUT = 512


def _sigmoid(x):
    return 1.0 / (1.0 + jnp.exp(-x))


def _silu(x):
    return x * _sigmoid(x)


def _inproj_kernel(x_ref, gpre_ref, w_ref, glu_ref, zc_ref, q_ref, k_ref, v_ref, za_ref, g_ref):
    x = x_ref[...]
    ms = jnp.mean(x * x, axis=-1, keepdims=True)
    h = (x * lax.rsqrt(ms + EPS) * gpre_ref[...]).astype(BF16)

    def proj(c0, width):
        return jnp.dot(h, w_ref[:, c0:c0 + width], preferred_element_type=F32)

    a = proj(0, D_CONV)
    b = proj(D_CONV, D_CONV)
    glu_ref[...] = (a * _sigmoid(b)).astype(BF16)
    zc_ref[...] = _silu(proj(2 * D_CONV, D_CONV)).astype(BF16)
    c0 = 3 * D_CONV
    q_ref[...] = (proj(c0, D_QK) * (LOG2E / math.sqrt(HEAD_DIM))).astype(BF16)
    k_ref[...] = proj(c0 + D_QK, D_QK).astype(BF16)
    v_ref[...] = proj(c0 + 2 * D_QK, D_ATTN).astype(BF16)
    za_ref[...] = _silu(proj(c0 + 2 * D_QK + D_ATTN, D_ATTN)).astype(BF16)
    c0 = c0 + 2 * D_QK + 2 * D_ATTN
    for j in range(2 * D_MODEL // 512):
        g_ref[:, j * 512:(j + 1) * 512] = _sigmoid(proj(c0 + j * 512, 512)).astype(BF16)


def _inproj(x2, gpre, w_in):
    T = x2.shape[0]
    d_in = w_in.shape[1]
    row = lambda i: (i, 0)
    fixed = lambda i: (0, 0)
    widths = (D_CONV, D_CONV, D_QK, D_QK, D_ATTN, D_ATTN, 2 * D_MODEL)
    return pl.pallas_call(
        _inproj_kernel,
        grid=(T // TM_IN,),
        in_specs=[
            pl.BlockSpec((TM_IN, D_MODEL), row),
            pl.BlockSpec((1, D_MODEL), fixed),
            pl.BlockSpec((D_MODEL, d_in), fixed, pipeline_mode=pl.Buffered(1)),
        ],
        out_specs=[pl.BlockSpec((TM_IN, w), row) for w in widths],
        out_shape=[jax.ShapeDtypeStruct((T, w), BF16) for w in widths],
        compiler_params=pltpu.CompilerParams(
            dimension_semantics=("arbitrary",), vmem_limit_bytes=VMEM_LIMIT_BYTES),
        name="inproj",
    )(x2, gpre, w_in)


def _conv_kernel(glu_ref, halo_ref, zc_ref, w_ref, b_ref, lng_ref, lnb_ref, o_ref, xpad_ref, acc_ref):
    i = pl.program_id(1)
    halo = halo_ref[0].astype(F32)
    xpad_ref[0:CONV_HALO, :] = jnp.where(i == 0, 0.0, halo)
    xpad_ref[CONV_HALO:, :] = glu_ref[0].astype(F32)

    shift = CONV_HALO - (CONV_WIDTH - 1)
    for r0 in range(0, TS_CONV, CONV_ROWS):
        for c0 in range(0, D_CONV, 128):
            acc = jnp.zeros((CONV_ROWS, 128), F32)
            for j in range(CONV_WIDTH):
                xs = xpad_ref[r0 + shift + j:r0 + shift + j + CONV_ROWS, c0:c0 + 128]
                acc = acc + xs * w_ref[j:j + 1, c0:c0 + 128]
            acc_ref[r0:r0 + CONV_ROWS, c0:c0 + 128] = acc

    y = acc_ref[...] + b_ref[...]
    mu = jnp.mean(y, axis=-1, keepdims=True)
    yc = y - mu
    var = jnp.mean(yc * yc, axis=-1, keepdims=True)
    yn = yc * lax.rsqrt(var + EPS) * lng_ref[...] + lnb_ref[...]
    o_ref[0] = (_silu(yn) * zc_ref[0].astype(F32)).astype(BF16)


def _conv(glu, zc, conv_w, conv_b, ln_g, ln_b):
    B, S, _ = glu.shape
    blocks_per_halo = TS_CONV // CONV_HALO
    cur = lambda b, i: (b, i, 0)
    prev = lambda b, i: (b, jnp.maximum(i * blocks_per_halo - 1, 0), 0)
    fixed = lambda b, i: (0, 0)
    return pl.pallas_call(
        _conv_kernel,
        grid=(B, S // TS_CONV),
        in_specs=[
            pl.BlockSpec((1, TS_CONV, D_CONV), cur),
            pl.BlockSpec((1, CONV_HALO, D_CONV), prev),
            pl.BlockSpec((1, TS_CONV, D_CONV), cur),
            pl.BlockSpec((CONV_WIDTH, D_CONV), fixed),
            pl.BlockSpec((1, D_CONV), fixed),
            pl.BlockSpec((1, D_CONV), fixed),
            pl.BlockSpec((1, D_CONV), fixed),
        ],
        out_specs=pl.BlockSpec((1, TS_CONV, D_CONV), cur),
        out_shape=jax.ShapeDtypeStruct((B, S, D_CONV), BF16),
        scratch_shapes=[
            pltpu.VMEM((TS_CONV + CONV_HALO, D_CONV), F32),
            pltpu.VMEM((TS_CONV, D_CONV), F32),
        ],
        compiler_params=pltpu.CompilerParams(
            dimension_semantics=("arbitrary", "arbitrary"), vmem_limit_bytes=VMEM_LIMIT_BYTES),
        name="conv",
    )(glu, glu, zc, conv_w, conv_b, ln_g, ln_b)


def _dot_nt(a, b):
    return lax.dot_general(a, b, (((1,), (1,)), ((), ())), preferred_element_type=F32)


def _attn_kernel(q_ref, k_ref, v_ref, za_ref, lq1_ref, lk1_ref, lq2_ref, lk2_ref, subg_ref,
                 o_ref, acc1_ref, acc2_ref, m1_ref, m2_ref):
    i = pl.program_id(1)

    lam = (jnp.exp(jnp.sum(lq1_ref[...] * lk1_ref[...], axis=-1, keepdims=True))
           - jnp.exp(jnp.sum(lq2_ref[...] * lk2_ref[...], axis=-1, keepdims=True))
           + LAMBDA_INIT)

    tl = lax.broadcasted_iota(jnp.int32, (TQ, TK), 0)
    sl = lax.broadcasted_iota(jnp.int32, (TQ, TK), 1)
    allowed = (sl // CHUNK) <= (tl // CHUNK)
    diag_rel = (tl - jnp.abs(tl - sl)).astype(F32)
    col = lax.broadcasted_iota(jnp.int32, (1, TK), 1)
    lane = lax.broadcasted_iota(jnp.int32, (TQ, 2 * HEAD_DIM), 1)
    ones = jnp.ones((TK, V_DIM), BF16)

    for h in range(N_HEADS):
        hs = slice(h * V_DIM, (h + 1) * V_DIM)
        slope2 = SLOPES[h] * LOG2E
        q = q_ref[0, :, hs]
        qa = jnp.where(lane < HEAD_DIM, q, jnp.zeros_like(q))
        qb = jnp.where(lane >= HEAD_DIM, q, jnp.zeros_like(q))

        def block(j0, bias, first):
            kb = k_ref[0, pl.ds(j0, TK), hs]
            vb = jnp.concatenate([v_ref[0, pl.ds(j0, TK), hs], ones], axis=1)
            for qm, acc_ref, m_ref in ((qa, acc1_ref, m1_ref), (qb, acc2_ref, m2_ref)):
                s = _dot_nt(qm, kb) + bias
                mb = jnp.max(s, axis=-1, keepdims=True)
                if first:
                    m_new = mb
                else:
                    m_old = m_ref[...]
                    m_new = jnp.maximum(m_old, mb)
                p = jnp.exp2(s - m_new).astype(BF16)
                pv = jnp.dot(p, vb, preferred_element_type=F32)
                if first:
                    acc_ref[...] = pv
                else:
                    acc_ref[...] = acc_ref[...] * jnp.exp2(m_old - m_new) + pv
                m_ref[...] = m_new

        t0 = pl.multiple_of(i * TQ, TQ)
        block(t0, jnp.where(allowed, slope2 * diag_rel, -jnp.inf), True)

        def body(j, carry):
            j0 = pl.multiple_of(j * TK, TK)
            block(j0, slope2 * (col + (j0 - t0)).astype(F32), False)
            return carry

        lax.fori_loop(0, i, body, 0)

        a1 = acc1_ref[...]
        a2 = acc2_ref[...]
        o = a1[:, :V_DIM] / a1[:, V_DIM:] - lam * (a2[:, :V_DIM] / a2[:, V_DIM:])
        o = o * lax.rsqrt(jnp.mean(o * o, axis=-1, keepdims=True) + EPS)
        o = o * subg_ref[...] * (1.0 - LAMBDA_INIT)
        o_ref[0, :, hs] = (o * za_ref[0, :, hs].astype(F32)).astype(BF16)


def _attn(q, k, v, za, lq1, lk1, lq2, lk2, subg):
    B, S, _ = q.shape
    blk = lambda b, i: (b, i, 0)
    whole = lambda b, i: (b, 0, 0)
    fixed = lambda b, i: (0, 0)
    return pl.pallas_call(
        _attn_kernel,
        grid=(B, S // TQ),
        in_specs=[
            pl.BlockSpec((1, TQ, D_QK), blk),
            pl.BlockSpec((1, S, D_QK), whole),
            pl.BlockSpec((1, S, D_ATTN), whole),
            pl.BlockSpec((1, TQ, D_ATTN), blk),
            pl.BlockSpec((1, HEAD_DIM), fixed),
            pl.BlockSpec((1, HEAD_DIM), fixed),
            pl.BlockSpec((1, HEAD_DIM), fixed),
            pl.BlockSpec((1, HEAD_DIM), fixed),
            pl.BlockSpec((1, V_DIM), fixed),
        ],
        out_specs=pl.BlockSpec((1, TQ, D_ATTN), blk),
        out_shape=jax.ShapeDtypeStruct((B, S, D_ATTN), BF16),
        scratch_shapes=[
            pltpu.VMEM((TQ, 2 * V_DIM), F32),
            pltpu.VMEM((TQ, 2 * V_DIM), F32),
            pltpu.VMEM((TQ, 1), F32),
            pltpu.VMEM((TQ, 1), F32),
        ],
        compiler_params=pltpu.CompilerParams(
            dimension_semantics=("arbitrary", "arbitrary"), vmem_limit_bytes=VMEM_LIMIT_BYTES),
        name="attn",
    )(q, k, v, za, lq1, lk1, lq2, lk2, subg)


def _out_kernel(x_ref, hc_ref, ha_ref, g_ref, wc_ref, wa_ref, wo_ref, gpost_ref, o_ref):
    yc = jnp.dot(hc_ref[...], wc_ref[...], preferred_element_type=F32)
    ya = jnp.dot(ha_ref[...], wa_ref[...], preferred_element_type=F32)
    mixed = (g_ref[:, :D_MODEL].astype(F32) * yc + g_ref[:, D_MODEL:].astype(F32) * ya).astype(BF16)
    out = jnp.dot(mixed, wo_ref[...], preferred_element_type=F32)
    ms = jnp.mean(out * out, axis=-1, keepdims=True)
    o_ref[...] = x_ref[...] + out * lax.rsqrt(ms + EPS) * gpost_ref[...]


def _out(x2, hc, ha, g, wc, wa, wo, gpost):
    T = x2.shape[0]
    row = lambda i: (i, 0)
    fixed = lambda i: (0, 0)
    return pl.pallas_call(
        _out_kernel,
        grid=(T // TM_OUT,),
        in_specs=[
            pl.BlockSpec((TM_OUT, D_MODEL), row),
            pl.BlockSpec((TM_OUT, D_CONV), row),
            pl.BlockSpec((TM_OUT, D_ATTN), row),
            pl.BlockSpec((TM_OUT, 2 * D_MODEL), row),
            pl.BlockSpec((D_CONV, D_MODEL), fixed),
            pl.BlockSpec((D_ATTN, D_MODEL), fixed),
            pl.BlockSpec((D_MODEL, D_MODEL), fixed),
            pl.BlockSpec((1, D_MODEL), fixed),
        ],
        out_specs=pl.BlockSpec((TM_OUT, D_MODEL), row),
        out_shape=jax.ShapeDtypeStruct((T, D_MODEL), F32),
        compiler_params=pltpu.CompilerParams(
            dimension_semantics=("arbitrary",), vmem_limit_bytes=VMEM_LIMIT_BYTES),
        name="out",
    )(x2, hc, ha, g, wc, wa, wo, gpost)


def kernel(x, w_in, conv_w, conv_b, conv_ln_g, conv_ln_b, w_conv_proj, lambda_q1, lambda_k1,
           lambda_q2, lambda_k2, subln_g, w_attn_proj, w_out, norm_pre_g, norm_post_g):
    B, S, D = x.shape
    assert (D, w_in.shape[0]) == (D_MODEL, 1) and S % max(TQ, TS_CONV) == 0 and (B * S) % TM_IN == 0
    T = B * S
    x2 = x.reshape(T, D)
    glu, zc, q, k, v, za, g = _inproj(x2, norm_pre_g, w_in[0].astype(BF16))
    seq = lambda a: a.reshape(B, S, a.shape[-1])
    hc = _conv(seq(glu), seq(zc), conv_w[0], conv_b, conv_ln_g, conv_ln_b)
    ha = _attn(seq(q), seq(k), seq(v), seq(za), lambda_q1, lambda_k1, lambda_q2, lambda_k2, subln_g)
    y = _out(x2, hc.reshape(T, D_CONV), ha.reshape(T, D_ATTN), g,
             w_conv_proj[0].astype(BF16), w_attn_proj[0].astype(BF16), w_out[0].astype(BF16),
             norm_post_g)
    return y.reshape(B, S, D)
```

```python
import functools
import math

import jax
import jax.numpy as jnp
from jax import lax
from jax.experimental import pallas as pl
from jax.experimental.pallas import tpu as pltpu

D_MODEL = 1024
CHUNK = 64
CONV_WIDTH = 31
D_CONV = 512
N_HEADS = 4
HEAD_DIM = 64
V_DIM = 128
D_ATTN = 512
D_QK = 512
EPS = 1e-6
LAMBDA_INIT = 0.8 - 0.6 * math.exp(-0.3 * 0)
LOG2E = math.log2(math.e)
SLOPES = tuple(2.0 ** (-8.0 * (h + 1) / N_HEADS) for h in range(N_HEADS))

BF16 = jnp.bfloat16
F32 = jnp.float32

VMEM_LIMIT_BYTES = 56 * 1024 * 1024

TM_IN = 512
TS_CONV = 256
CONV_HALO = 32
CONV_ROWS = 64
TQ = 256
TK = 256
TM_OUT = 512


def _sigmoid(x):
    return 1.0 / (1.0 + jnp.exp(-x))


def _silu(x):
    return x * _sigmoid(x)


def _inproj_kernel(x_ref, gpre_ref, w_ref, glu_ref, zc_ref, q_ref, k_ref, v_ref, za_ref, g_ref):
    x = x_ref[...]
    ms = jnp.mean(x * x, axis=-1, keepdims=True)
    h = (x * lax.rsqrt(ms + EPS) * gpre_ref[...]).astype(BF16)

    def proj(c0, width):
        return jnp.dot(h, w_ref[:, c0:c0 + width], preferred_element_type=F32)

    a = proj(0, D_CONV)
    b = proj(D_CONV, D_CONV)
    glu_ref[...] = (a * _sigmoid(b)).astype(BF16)
    zc_ref[...] = _silu(proj(2 * D_CONV, D_CONV)).astype(BF16)
    c0 = 3 * D_CONV
    q_ref[...] = (proj(c0, D_QK) * (LOG2E / math.sqrt(HEAD_DIM))).astype(BF16)
    k_ref[...] = proj(c0 + D_QK, D_QK).astype(BF16)
    v_ref[...] = proj(c0 + 2 * D_QK, D_ATTN).astype(BF16)
    za_ref[...] = _silu(proj(c0 + 2 * D_QK + D_ATTN, D_ATTN)).astype(BF16)
    c0 = c0 + 2 * D_QK + 2 * D_ATTN
    for j in range(2 * D_MODEL // 512):
        g_ref[:, j * 512:(j + 1) * 512] = _sigmoid(proj(c0 + j * 512, 512)).astype(BF16)


def _inproj(x2, gpre, w_in):
    T = x2.shape[0]
    d_in = w_in.shape[1]
    row = lambda i: (i, 0)
    fixed = lambda i: (0, 0)
    widths = (D_CONV, D_CONV, D_QK, D_QK, D_ATTN, D_ATTN, 2 * D_MODEL)
    return pl.pallas_call(
        _inproj_kernel,
        grid=(T // TM_IN,),
        in_specs=[
            pl.BlockSpec((TM_IN, D_MODEL), row),
            pl.BlockSpec((1, D_MODEL), fixed),
            pl.BlockSpec((D_MODEL, d_in), fixed, pipeline_mode=pl.Buffered(1)),
        ],
        out_specs=[pl.BlockSpec((TM_IN, w), row) for w in widths],
        out_shape=[jax.ShapeDtypeStruct((T, w), BF16) for w in widths],
        compiler_params=pltpu.CompilerParams(
            dimension_semantics=("arbitrary",), vmem_limit_bytes=VMEM_LIMIT_BYTES),
        name="inproj",
    )(x2, gpre, w_in)


def _conv_kernel(glu_ref, halo_ref, zc_ref, w_ref, b_ref, lng_ref, lnb_ref, o_ref, xpad_ref, acc_ref):
    i = pl.program_id(1)
    halo = halo_ref[0].astype(F32)
    xpad_ref[0:CONV_HALO, :] = jnp.where(i == 0, 0.0, halo)
    xpad_ref[CONV_HALO:, :] = glu_ref[0].astype(F32)

    shift = CONV_HALO - (CONV_WIDTH - 1)
    for r0 in range(0, TS_CONV, CONV_ROWS):
        for c0 in range(0, D_CONV, 128):
            acc = jnp.zeros((CONV_ROWS, 128), F32)
            for j in range(CONV_WIDTH):
                xs = xpad_ref[r0 + shift + j:r0 + shift + j + CONV_ROWS, c0:c0 + 128]
                acc = acc + xs * w_ref[j:j + 1, c0:c0 + 128]
            acc_ref[r0:r0 + CONV_ROWS, c0:c0 + 128] = acc

    y = acc_ref[...] + b_ref[...]
    mu = jnp.mean(y, axis=-1, keepdims=True)
    yc = y - mu
    var = jnp.mean(yc * yc, axis=-1, keepdims=True)
    yn = yc * lax.rsqrt(var + EPS) * lng_ref[...] + lnb_ref[...]
    o_ref[0] = (_silu(yn) * zc_ref[0].astype(F32)).astype(BF16)


def _conv(glu, zc, conv_w, conv_b, ln_g, ln_b):
    B, S, _ = glu.shape
    blocks_per_halo = TS_CONV // CONV_HALO
    cur = lambda b, i: (b, i, 0)
    prev = lambda b, i: (b, jnp.maximum(i * blocks_per_halo - 1, 0), 0)
    fixed = lambda b, i: (0, 0)
    return pl.pallas_call(
        _conv_kernel,
        grid=(B, S // TS_CONV),
        in_specs=[
            pl.BlockSpec((1, TS_CONV, D_CONV), cur),
            pl.BlockSpec((1, CONV_HALO, D_CONV), prev),
            pl.BlockSpec((1, TS_CONV, D_CONV), cur),
            pl.BlockSpec((CONV_WIDTH, D_CONV), fixed),
            pl.BlockSpec((1, D_CONV), fixed),
            pl.BlockSpec((1, D_CONV), fixed),
            pl.BlockSpec((1, D_CONV), fixed),
        ],
        out_specs=pl.BlockSpec((1, TS_CONV, D_CONV), cur),
        out_shape=jax.ShapeDtypeStruct((B, S, D_CONV), BF16),
        scratch_shapes=[
            pltpu.VMEM((TS_CONV + CONV_HALO, D_CONV), F32),
            pltpu.VMEM((TS_CONV, D_CONV), F32),
        ],
        compiler_params=pltpu.CompilerParams(
            dimension_semantics=("arbitrary", "arbitrary"), vmem_limit_bytes=VMEM_LIMIT_BYTES),
        name="conv",
    )(glu, glu, zc, conv_w, conv_b, ln_g, ln_b)


def _dot_nt(a, b):
    return lax.dot_general(a, b, (((1,), (1,)), ((), ())), preferred_element_type=F32)


def _attn_kernel(q_ref, k_ref, v_ref, za_ref, lq1_ref, lk1_ref, lq2_ref, lk2_ref, subg_ref,
                 o_ref, q2_ref, acc_ref, m_ref):
    i = pl.program_id(1)
    t0 = pl.multiple_of(i * TQ, TQ)

    tl = lax.broadcasted_iota(jnp.int32, (2 * TQ, TK), 0) % TQ
    sl = lax.broadcasted_iota(jnp.int32, (2 * TQ, TK), 1)
    allowed = (sl // CHUNK) <= (tl // CHUNK)
    diag_rel = (tl - jnp.abs(tl - sl)).astype(F32)
    col = lax.broadcasted_iota(jnp.int32, (1, TK), 1)
    lane = lax.broadcasted_iota(jnp.int32, (TQ, 2 * HEAD_DIM), 1)
    ones = jnp.ones((TK, V_DIM), BF16)

    for h in range(N_HEADS):
        q = q_ref[0, :, h * V_DIM:(h + 1) * V_DIM]
        q2_ref[h, 0:TQ, :] = jnp.where(lane < HEAD_DIM, q, jnp.zeros_like(q))
        q2_ref[h, TQ:, :] = jnp.where(lane >= HEAD_DIM, q, jnp.zeros_like(q))

    def block(h, j0, bias, first):
        hs = slice(h * V_DIM, (h + 1) * V_DIM)
        kb = k_ref[0, pl.ds(j0, TK), hs]
        vb = jnp.concatenate([v_ref[0, pl.ds(j0, TK), hs], ones], axis=1)
        s = _dot_nt(q2_ref[h], kb) + bias
        mb = jnp.broadcast_to(jnp.max(s, axis=-1, keepdims=True), (2 * TQ, V_DIM))
        if first:
            m_new = mb
        else:
            m_old = m_ref[h]
            m_new = jnp.maximum(m_old, mb)
        p = jnp.exp2(s - jnp.concatenate([m_new] * (TK // V_DIM), axis=1)).astype(BF16)
        pv = jnp.dot(p, vb, preferred_element_type=F32)
        if first:
            acc_ref[h] = pv
        else:
            alpha = jnp.exp2(m_old - m_new)
            acc_ref[h] = acc_ref[h] * jnp.concatenate([alpha, alpha], axis=1) + pv
        m_ref[h] = m_new

    for h in range(N_HEADS):
        block(h, t0, jnp.where(allowed, (SLOPES[h] * LOG2E) * diag_rel, -jnp.inf), True)

    def body(j, carry):
        j0 = pl.multiple_of(j * TK, TK)
        rel = (col + (j0 - t0)).astype(F32)
        for h in range(N_HEADS):
            block(h, j0, (SLOPES[h] * LOG2E) * rel, False)
        return carry

    lax.fori_loop(0, i, body, 0)

    lam = (jnp.exp(jnp.sum(lq1_ref[...] * lk1_ref[...], axis=-1, keepdims=True))
           - jnp.exp(jnp.sum(lq2_ref[...] * lk2_ref[...], axis=-1, keepdims=True))
           + LAMBDA_INIT)
    for h in range(N_HEADS):
        hs = slice(h * V_DIM, (h + 1) * V_DIM)
        a1 = acc_ref[h, 0:TQ, :]
        a2 = acc_ref[h, TQ:, :]
        o = a1[:, :V_DIM] / a1[:, V_DIM:] - lam * (a2[:, :V_DIM] / a2[:, V_DIM:])
        o = o * lax.rsqrt(jnp.mean(o * o, axis=-1, keepdims=True) + EPS)
        o = o * subg_ref[...] * (1.0 - LAMBDA_INIT)
        o_ref[0, :, hs] = (o * za_ref[0, :, hs].astype(F32)).astype(BF16)


def _attn(q, k, v, za, lq1, lk1, lq2, lk2, subg):
    B, S, _ = q.shape
    blk = lambda b, i: (b, i, 0)
    whole = lambda b, i: (b, 0, 0)
    fixed = lambda b, i: (0, 0)
    return pl.pallas_call(
        _attn_kernel,
        grid=(B, S // TQ),
        in_specs=[
            pl.BlockSpec((1, TQ, D_QK), blk),
            pl.BlockSpec((1, S, D_QK), whole),
            pl.BlockSpec((1, S, D_ATTN), whole),
            pl.BlockSpec((1, TQ, D_ATTN), blk),
            pl.BlockSpec((1, HEAD_DIM), fixed),
            pl.BlockSpec((1, HEAD_DIM), fixed),
            pl.BlockSpec((1, HEAD_DIM), fixed),
            pl.BlockSpec((1, HEAD_DIM), fixed),
            pl.BlockSpec((1, V_DIM), fixed),
        ],
        out_specs=pl.BlockSpec((1, TQ, D_ATTN), blk),
        out_shape=jax.ShapeDtypeStruct((B, S, D_ATTN), BF16),
        scratch_shapes=[
            pltpu.VMEM((N_HEADS, 2 * TQ, 2 * HEAD_DIM), BF16),
            pltpu.VMEM((N_HEADS, 2 * TQ, 2 * V_DIM), F32),
            pltpu.VMEM((N_HEADS, 2 * TQ, V_DIM), F32),
        ],
        compiler_params=pltpu.CompilerParams(
            dimension_semantics=("arbitrary", "arbitrary"), vmem_limit_bytes=VMEM_LIMIT_BYTES),
        name="attn",
    )(q, k, v, za, lq1, lk1, lq2, lk2, subg)


def _out_kernel(x_ref, hc_ref, ha_ref, g_ref, wc_ref, wa_ref, wo_ref, gpost_ref, o_ref):
    yc = jnp.dot(hc_ref[...], wc_ref[...], preferred_element_type=F32)
    ya = jnp.dot(ha_ref[...], wa_ref[...], preferred_element_type=F32)
    mixed = (g_ref[:, :D_MODEL].astype(F32) * yc + g_ref[:, D_MODEL:].astype(F32) * ya).astype(BF16)
    out = jnp.dot(mixed, wo_ref[...], preferred_element_type=F32)
    ms = jnp.mean(out * out, axis=-1, keepdims=True)
    o_ref[...] = x_ref[...] + out * lax.rsqrt(ms + EPS) * gpost_ref[...]


def _out(x2, hc, ha, g, wc, wa, wo, gpost):
    T = x2.shape[0]
    row = lambda i: (i, 0)
    fixed = lambda i: (0, 0)
    return pl.pallas_call(
        _out_kernel,
        grid=(T // TM_OUT,),
        in_specs=[
            pl.BlockSpec((TM_OUT, D_MODEL), row),
            pl.BlockSpec((TM_OUT, D_CONV), row),
            pl.BlockSpec((TM_OUT, D_ATTN), row),
            pl.BlockSpec((TM_OUT, 2 * D_MODEL), row),
            pl.BlockSpec((D_CONV, D_MODEL), fixed),
            pl.BlockSpec((D_ATTN, D_MODEL), fixed),
            pl.BlockSpec((D_MODEL, D_MODEL), fixed),
            pl.BlockSpec((1, D_MODEL), fixed),
        ],
        out_specs=pl.BlockSpec((TM_OUT, D_MODEL), row),
        out_shape=jax.ShapeDtypeStruct((T, D_MODEL), F32),
        compiler_params=pltpu.CompilerParams(
            dimension_semantics=("arbitrary",), vmem_limit_bytes=VMEM_LIMIT_BYTES),
        name="out",
    )(x2, hc, ha, g, wc, wa, wo, gpost)


def kernel(x, w_in, conv_w, conv_b, conv_ln_g, conv_ln_b, w_conv_proj, lambda_q1, lambda_k1,
           lambda_q2, lambda_k2, subln_g, w_attn_proj, w_out, norm_pre_g, norm_post_g):
    B, S, D = x.shape
    assert (D, w_in.shape[0]) == (D_MODEL, 1) and S % max(TQ, TS_CONV) == 0 and (B * S) % TM_IN == 0
    T = B * S
    x2 = x.reshape(T, D)
    glu, zc, q, k, v, za, g = _inproj(x2, norm_pre_g, w_in[0].astype(BF16))
    seq = lambda a: a.reshape(B, S, a.shape[-1])
    hc = _conv(seq(glu), seq(zc), conv_w[0], conv_b, conv_ln_g, conv_ln_b)
    ha = _attn(seq(q), seq(k), seq(v), seq(za), lambda_q1, lambda_k1, lambda_q2, lambda_k2, subln_g)
    y = _out(x2, hc.reshape(T, D_CONV), ha.reshape(T, D_ATTN), g,
             w_conv_proj[0].astype(BF16), w_attn_proj[0].astype(BF16), w_out[0].astype(BF16),
             norm_post_g)
    return y.reshape(B, S, D)
```

```python
import math
import struct

import jax
import jax.numpy as jnp
from jax import lax
from jax.experimental import pallas as pl
from jax.experimental.pallas import tpu as pltpu

D_MODEL = 1024
CHUNK = 64
CONV_WIDTH = 31
D_CONV = 512
N_HEADS = 4
HEAD_DIM = 64
V_DIM = 128
D_ATTN = 512
D_QK = 512
EPS = 1e-6
LAMBDA_INIT = 0.8 - 0.6 * math.exp(-0.3 * 0)
LOG2E = math.log2(math.e)
SLOPES = tuple(2.0 ** (-8.0 * (h + 1) / N_HEADS) for h in range(N_HEADS))

BF16 = jnp.bfloat16
F32 = jnp.float32

VMEM_LIMIT_BYTES = 56 * 1024 * 1024

TM_IN = 512
TS_CONV = 256
CONV_HALO = 32
CONV_ROWS = 64
TQ = 256
TK = 256
TM_OUT = 512


def _sigmoid(x):
    return 1.0 / (1.0 + jnp.exp(-x))


def _silu(x):
    return x * _sigmoid(x)


def _inproj_kernel(x_ref, gpre_ref, w_ref, glu_ref, zc_ref, qt_ref, k_ref, vt_ref, za_ref, g_ref):
    x = x_ref[...]
    ms = jnp.mean(x * x, axis=-1, keepdims=True)
    h = (x * lax.rsqrt(ms + EPS) * gpre_ref[...]).astype(BF16)

    def proj(c0, width):
        return jnp.dot(h, w_ref[:, c0:c0 + width], preferred_element_type=F32)

    def store_transposed(t_ref, y):
        for blk in range(TM_IN // TK):
            for hd in range(N_HEADS):
                t_ref[blk, hd] = y[blk * TK:(blk + 1) * TK, hd * V_DIM:(hd + 1) * V_DIM].T.astype(BF16)

    a = proj(0, D_CONV)
    b = proj(D_CONV, D_CONV)
    glu_ref[...] = (a * _sigmoid(b)).astype(BF16)
    zc_ref[...] = _silu(proj(2 * D_CONV, D_CONV)).astype(BF16)
    c0 = 3 * D_CONV
    store_transposed(qt_ref, proj(c0, D_QK) * (LOG2E / math.sqrt(HEAD_DIM)))
    k_ref[...] = proj(c0 + D_QK, D_QK).astype(BF16)
    store_transposed(vt_ref, proj(c0 + 2 * D_QK, D_ATTN))
    za_ref[...] = _silu(proj(c0 + 2 * D_QK + D_ATTN, D_ATTN)).astype(BF16)
    c0 = c0 + 2 * D_QK + 2 * D_ATTN
    for j in range(2 * D_MODEL // 512):
        g_ref[:, j * 512:(j + 1) * 512] = _sigmoid(proj(c0 + j * 512, 512)).astype(BF16)


def _inproj(x2, gpre, w_in):
    T = x2.shape[0]
    d_in = w_in.shape[1]
    row = lambda i: (i, 0)
    fixed = lambda i: (0, 0)
    tr_shape = (T // TK, N_HEADS, V_DIM, TK)
    tr_spec = pl.BlockSpec((TM_IN // TK, N_HEADS, V_DIM, TK), lambda i: (i, 0, 0, 0))
    rows = lambda w: (pl.BlockSpec((TM_IN, w), row), jax.ShapeDtypeStruct((T, w), BF16))
    tr = (tr_spec, jax.ShapeDtypeStruct(tr_shape, BF16))
    outs = (rows(D_CONV), rows(D_CONV), tr, rows(D_QK), tr, rows(D_ATTN), rows(2 * D_MODEL))
    return pl.pallas_call(
        _inproj_kernel,
        grid=(T // TM_IN,),
        in_specs=[
            pl.BlockSpec((TM_IN, D_MODEL), row),
            pl.BlockSpec((1, D_MODEL), fixed),
            pl.BlockSpec((D_MODEL, d_in), fixed, pipeline_mode=pl.Buffered(1)),
        ],
        out_specs=[o[0] for o in outs],
        out_shape=[o[1] for o in outs],
        compiler_params=pltpu.CompilerParams(
            dimension_semantics=("arbitrary",), vmem_limit_bytes=VMEM_LIMIT_BYTES),
        name="inproj",
    )(x2, gpre, w_in)


def _conv_kernel(glu_ref, halo_ref, zc_ref, w_ref, b_ref, lng_ref, lnb_ref, o_ref, xpad_ref, acc_ref):
    i = pl.program_id(1)
    halo = jnp.where(i == 0, 0.0, halo_ref[0].astype(F32))
    cur = glu_ref[0].astype(F32)
    for c in range(D_CONV // 128):
        xpad_ref[c, 0:CONV_HALO, :] = halo[:, c * 128:(c + 1) * 128]
        xpad_ref[c, CONV_HALO:, :] = cur[:, c * 128:(c + 1) * 128]

    shift = CONV_HALO - (CONV_WIDTH - 1)

    def rows(r, carry):
        r0 = pl.multiple_of(r * CONV_ROWS, CONV_ROWS)
        for c in range(D_CONV // 128):
            acc = jnp.zeros((CONV_ROWS, 128), F32)
            for j in range(CONV_WIDTH):
                xs = xpad_ref[c, pl.ds(r0 + (shift + j), CONV_ROWS), :]
                acc = acc + xs * w_ref[j:j + 1, c * 128:(c + 1) * 128]
            acc_ref[pl.ds(r0, CONV_ROWS), c * 128:(c + 1) * 128] = acc
        return carry

    lax.fori_loop(0, TS_CONV // CONV_ROWS, rows, 0)

    y = acc_ref[...] + b_ref[...]
    mu = jnp.mean(y, axis=-1, keepdims=True)
    yc = y - mu
    var = jnp.mean(yc * yc, axis=-1, keepdims=True)
    yn = yc * lax.rsqrt(var + EPS) * lng_ref[...] + lnb_ref[...]
    o_ref[0] = (_silu(yn) * zc_ref[0].astype(F32)).astype(BF16)


def _conv(glu, zc, conv_w, conv_b, ln_g, ln_b):
    B, S, _ = glu.shape
    blocks_per_halo = TS_CONV // CONV_HALO
    cur = lambda b, i: (b, i, 0)
    prev = lambda b, i: (b, jnp.maximum(i * blocks_per_halo - 1, 0), 0)
    fixed = lambda b, i: (0, 0)
    return pl.pallas_call(
        _conv_kernel,
        grid=(B, S // TS_CONV),
        in_specs=[
            pl.BlockSpec((1, TS_CONV, D_CONV), cur),
            pl.BlockSpec((1, CONV_HALO, D_CONV), prev),
            pl.BlockSpec((1, TS_CONV, D_CONV), cur),
            pl.BlockSpec((CONV_WIDTH, D_CONV), fixed),
            pl.BlockSpec((1, D_CONV), fixed),
            pl.BlockSpec((1, D_CONV), fixed),
            pl.BlockSpec((1, D_CONV), fixed),
        ],
        out_specs=pl.BlockSpec((1, TS_CONV, D_CONV), cur),
        out_shape=jax.ShapeDtypeStruct((B, S, D_CONV), BF16),
        scratch_shapes=[
            pltpu.VMEM((D_CONV // 128, TS_CONV + CONV_HALO, 128), F32),
            pltpu.VMEM((TS_CONV, D_CONV), F32),
        ],
        compiler_params=pltpu.CompilerParams(
            dimension_semantics=("arbitrary", "arbitrary"), vmem_limit_bytes=VMEM_LIMIT_BYTES),
        name="conv",
    )(glu, glu, zc, conv_w, conv_b, ln_g, ln_b)


def _bf16_round(x):
    bits = struct.unpack("<I", struct.pack("<f", x))[0]
    bits = (bits + 0x7FFF + ((bits >> 16) & 1)) & 0xFFFF0000
    return struct.unpack("<f", struct.pack("<I", bits))[0]


def _bf16_split3(c):
    c1 = _bf16_round(c)
    c2 = _bf16_round(c - c1)
    c3 = _bf16_round(c - c1 - c2)
    return c1, c2, c3


def _attn_kernel(qt_ref, k_ref, vt_ref, za_ref, lq1_ref, lk1_ref, lq2_ref, lk2_ref, subg_ref,
                 o_ref, qaug_ref, e_ref, tile_ref, acc_ref, m_ref, alpha_ref, s_ref, p_ref):
    i = pl.program_id(1)
    t0 = i * TQ
    slopes2 = [s * LOG2E for s in SLOPES]

    @pl.when((pl.program_id(0) == 0) & (i == 0))
    def _():
        sl = lax.broadcasted_iota(jnp.int32, (TK, 2 * TQ), 0)
        tl = lax.broadcasted_iota(jnp.int32, (TK, 2 * TQ), 1) % TQ
        allowed = (sl // CHUNK) <= (tl // CHUNK)
        rel = (tl - jnp.abs(tl - sl) - sl).astype(F32)
        crow = lax.broadcasted_iota(jnp.int32, (2 * HEAD_DIM, 2 * TQ), 0)
        for h in range(N_HEADS):
            tile_ref[h] = jnp.where(allowed, slopes2[h] * rel, -jnp.inf)
            c1, c2, c3 = _bf16_split3(slopes2[h])
            consts = jnp.where(crow == 0, c1, jnp.where(crow == 1, c2, jnp.where(crow == 2, c3, 0.0)))
            qaug_ref[h, 2 * HEAD_DIM:, :] = consts.astype(BF16)
        lane = lax.broadcasted_iota(jnp.int32, (TK, 2 * HEAD_DIM), 1)
        row = lax.broadcasted_iota(jnp.int32, (TK, 2 * HEAD_DIM), 0)
        e_ref[...] = jnp.where(lane < 3, row, 0).astype(F32).astype(BF16)

    zeros = jnp.zeros((HEAD_DIM, TQ), BF16)
    for h in range(N_HEADS):
        qt = qt_ref[0, h]
        qaug_ref[h, 0:2 * HEAD_DIM, :] = jnp.concatenate(
            [jnp.concatenate([qt[:HEAD_DIM], zeros], axis=0),
             jnp.concatenate([zeros, qt[HEAD_DIM:]], axis=0)], axis=1)

    ones = jnp.ones((16, TK), BF16)

    def scores(h, j):
        j0 = pl.multiple_of(j * TK, TK)
        lhs = jnp.concatenate([k_ref[0, pl.ds(j0, TK), h * V_DIM:(h + 1) * V_DIM], e_ref[...]], axis=1)
        return jnp.dot(lhs, qaug_ref[h], preferred_element_type=F32)

    def softmax(h, j, first):
        s = s_ref[h]
        if first:
            s = s + tile_ref[h]
            m_new = jnp.max(s, axis=0, keepdims=True)
            base = m_new
            alpha_ref[h] = jnp.zeros((1, 2 * TQ), F32)
        else:
            shift = slopes2[h] * jnp.full((1, 2 * TQ), j * TK - t0, jnp.int32).astype(F32)
            m_old = m_ref[h]
            m_new = jnp.maximum(m_old, jnp.max(s, axis=0, keepdims=True) + shift)
            base = m_new - shift
            alpha_ref[h] = jnp.exp2(m_old - m_new)
        p_ref[h] = jnp.exp2(s - base).astype(BF16)
        m_ref[h] = m_new

    def accumulate(h, j):
        lhs_v = jnp.concatenate([vt_ref[j, h], ones], axis=0)
        pv = jnp.dot(lhs_v, p_ref[h], preferred_element_type=F32)
        acc_ref[h] = acc_ref[h] * alpha_ref[h] + pv

    for h in range(N_HEADS):
        acc_ref[h] = jnp.zeros((V_DIM + 16, 2 * TQ), F32)
        s_ref[h] = scores(h, i)
    for h in range(N_HEADS):
        softmax(h, i, True)
        s_ref[h] = scores(h, 0)

    def body(t, carry):
        prev = jnp.where(t == 0, i, t - 1)
        nxt = jnp.minimum(t + 1, i)
        for h in range(N_HEADS):
            accumulate(h, prev)
        for h in range(N_HEADS):
            s_nxt = scores(h, nxt)
            softmax(h, t, False)
            s_ref[h] = s_nxt
        return carry

    lax.fori_loop(0, i, body, 0)
    for h in range(N_HEADS):
        accumulate(h, jnp.where(i == 0, i, i - 1))

    lam = (jnp.exp(jnp.sum(lq1_ref[...] * lk1_ref[...], axis=-1, keepdims=True))
           - jnp.exp(jnp.sum(lq2_ref[...] * lk2_ref[...], axis=-1, keepdims=True))
           + LAMBDA_INIT)
    for h in range(N_HEADS):
        hs = slice(h * V_DIM, (h + 1) * V_DIM)
        a = acc_ref[h]
        r = 1.0 / a[V_DIM:V_DIM + 1, :]
        o = (a[:V_DIM, :TQ] * r[:, :TQ] - lam * (a[:V_DIM, TQ:] * r[:, TQ:])).T
        o = o * lax.rsqrt(jnp.mean(o * o, axis=-1, keepdims=True) + EPS)
        o = o * subg_ref[...] * (1.0 - LAMBDA_INIT)
        o_ref[0, :, hs] = (o * za_ref[0, :, hs].astype(F32)).astype(BF16)


def _attn(qt, k, vt, za, lq1, lk1, lq2, lk2, subg):
    B, S, _ = k.shape
    nq = S // TQ
    blk = lambda b, i: (b, i, 0)
    whole = lambda b, i: (b, 0, 0)
    fixed = lambda b, i: (0, 0)
    return pl.pallas_call(
        _attn_kernel,
        grid=(B, nq),
        in_specs=[
            pl.BlockSpec((1, N_HEADS, 2 * HEAD_DIM, TQ), lambda b, i: (b * nq + i, 0, 0, 0)),
            pl.BlockSpec((1, S, D_QK), whole),
            pl.BlockSpec((S // TK, N_HEADS, V_DIM, TK), lambda b, i: (b, 0, 0, 0)),
            pl.BlockSpec((1, TQ, D_ATTN), blk),
            pl.BlockSpec((1, HEAD_DIM), fixed),
            pl.BlockSpec((1, HEAD_DIM), fixed),
            pl.BlockSpec((1, HEAD_DIM), fixed),
            pl.BlockSpec((1, HEAD_DIM), fixed),
            pl.BlockSpec((1, V_DIM), fixed),
        ],
        out_specs=pl.BlockSpec((1, TQ, D_ATTN), blk),
        out_shape=jax.ShapeDtypeStruct((B, S, D_ATTN), BF16),
        scratch_shapes=[
            pltpu.VMEM((N_HEADS, 4 * HEAD_DIM, 2 * TQ), BF16),
            pltpu.VMEM((TK, 2 * HEAD_DIM), BF16),
            pltpu.VMEM((N_HEADS, TK, 2 * TQ), F32),
            pltpu.VMEM((N_HEADS, V_DIM + 16, 2 * TQ), F32),
            pltpu.VMEM((N_HEADS, 1, 2 * TQ), F32),
            pltpu.VMEM((N_HEADS, 1, 2 * TQ), F32),
            pltpu.VMEM((N_HEADS, TK, 2 * TQ), F32),
            pltpu.VMEM((N_HEADS, TK, 2 * TQ), BF16),
        ],
        compiler_params=pltpu.CompilerParams(
            dimension_semantics=("arbitrary", "arbitrary"), vmem_limit_bytes=VMEM_LIMIT_BYTES),
        name="attn",
    )(qt, k, vt, za, lq1, lk1, lq2, lk2, subg)


def _out_kernel(x_ref, hc_ref, ha_ref, g_ref, wc_ref, wa_ref, wo_ref, gpost_ref, o_ref):
    yc = jnp.dot(hc_ref[...], wc_ref[...], preferred_element_type=F32)
    ya = jnp.dot(ha_ref[...], wa_ref[...], preferred_element_type=F32)
    mixed = (g_ref[:, :D_MODEL].astype(F32) * yc + g_ref[:, D_MODEL:].astype(F32) * ya).astype(BF16)
    out = jnp.dot(mixed, wo_ref[...], preferred_element_type=F32)
    ms = jnp.mean(out * out, axis=-1, keepdims=True)
    o_ref[...] = x_ref[...] + out * lax.rsqrt(ms + EPS) * gpost_ref[...]


def _out(x2, hc, ha, g, wc, wa, wo, gpost):
    T = x2.shape[0]
    row = lambda i: (i, 0)
    fixed = lambda i: (0, 0)
    return pl.pallas_call(
        _out_kernel,
        grid=(T // TM_OUT,),
        in_specs=[
            pl.BlockSpec((TM_OUT, D_MODEL), row),
            pl.BlockSpec((TM_OUT, D_CONV), row),
            pl.BlockSpec((TM_OUT, D_ATTN), row),
            pl.BlockSpec((TM_OUT, 2 * D_MODEL), row),
            pl.BlockSpec((D_CONV, D_MODEL), fixed),
            pl.BlockSpec((D_ATTN, D_MODEL), fixed),
            pl.BlockSpec((D_MODEL, D_MODEL), fixed),
            pl.BlockSpec((1, D_MODEL), fixed),
        ],
        out_specs=pl.BlockSpec((TM_OUT, D_MODEL), row),
        out_shape=jax.ShapeDtypeStruct((T, D_MODEL), F32),
        compiler_params=pltpu.CompilerParams(
            dimension_semantics=("arbitrary",), vmem_limit_bytes=VMEM_LIMIT_BYTES),
        name="out",
    )(x2, hc, ha, g, wc, wa, wo, gpost)


def kernel(x, w_in, conv_w, conv_b, conv_ln_g, conv_ln_b, w_conv_proj, lambda_q1, lambda_k1,
           lambda_q2, lambda_k2, subln_g, w_attn_proj, w_out, norm_pre_g, norm_post_g):
    B, S, D = x.shape
    assert (D, w_in.shape[0]) == (D_MODEL, 1) and TQ == TK
    assert S % max(TQ, TS_CONV) == 0 and (B * S) % TM_IN == 0
    T = B * S
    x2 = x.reshape(T, D)
    glu, zc, qt, k, vt, za, g = _inproj(x2, norm_pre_g, w_in[0].astype(BF16))
    seq = lambda a: a.reshape(B, S, a.shape[-1])
    hc = _conv(seq(glu), seq(zc), conv_w[0], conv_b, conv_ln_g, conv_ln_b)
    ha = _attn(qt, seq(k), vt, seq(za), lambda_q1, lambda_k1, lambda_q2, lambda_k2, subln_g)
    y = _out(x2, hc.reshape(T, D_CONV), ha.reshape(T, D_ATTN), g,
             w_conv_proj[0].astype(BF16), w_attn_proj[0].astype(BF16), w_out[0].astype(BF16),
             norm_post_g)
    return y.reshape(B, S, D)
```

```python
import math
import struct

import jax
import jax.numpy as jnp
from jax import lax
from jax.experimental import pallas as pl
from jax.experimental.pallas import tpu as pltpu

D_MODEL = 1024
CHUNK = 64
CONV_WIDTH = 31
D_CONV = 512
N_HEADS = 4
HEAD_DIM = 64
V_DIM = 128
D_ATTN = 512
D_QK = 512
EPS = 1e-6
LAMBDA_INIT = 0.8 - 0.6 * math.exp(-0.3 * 0)
LOG2E = math.log2(math.e)
SLOPES = tuple(2.0 ** (-8.0 * (h + 1) / N_HEADS) for h in range(N_HEADS))

BF16 = jnp.bfloat16
F32 = jnp.float32

VMEM_LIMIT_BYTES = 56 * 1024 * 1024

TM_IN = 1024
TS_CONV = 256
CONV_HALO = 32
CONV_ROWS = 64
TQ = 256
TK = 256
TM_OUT = 1024


def _sigmoid(x):
    return 1.0 / (1.0 + jnp.exp(-x))


def _silu(x):
    return x * _sigmoid(x)


def _inproj_kernel(x_ref, gpre_ref, w_ref, glu_ref, zc_ref, qt_ref, k_ref, vt_ref, za_ref, g_ref):
    x = x_ref[...]
    ms = jnp.mean(x * x, axis=-1, keepdims=True)
    h = (x * lax.rsqrt(ms + EPS) * gpre_ref[...]).astype(BF16)

    def proj(c0, width):
        return jnp.dot(h, w_ref[:, c0:c0 + width], preferred_element_type=F32)

    def store_transposed(t_ref, y):
        for blk in range(TM_IN // TK):
            for hd in range(N_HEADS):
                t_ref[blk, hd] = y[blk * TK:(blk + 1) * TK, hd * V_DIM:(hd + 1) * V_DIM].T.astype(BF16)

    a = proj(0, D_CONV)
    b = proj(D_CONV, D_CONV)
    glu_ref[...] = (a * _sigmoid(b)).astype(BF16)
    zc_ref[...] = _silu(proj(2 * D_CONV, D_CONV)).astype(BF16)
    c0 = 3 * D_CONV
    store_transposed(qt_ref, proj(c0, D_QK) * (LOG2E / math.sqrt(HEAD_DIM)))
    k_ref[...] = proj(c0 + D_QK, D_QK).astype(BF16)
    store_transposed(vt_ref, proj(c0 + 2 * D_QK, D_ATTN))
    za_ref[...] = _silu(proj(c0 + 2 * D_QK + D_ATTN, D_ATTN)).astype(BF16)
    c0 = c0 + 2 * D_QK + 2 * D_ATTN
    for j in range(2 * D_MODEL // 512):
        g_ref[:, j * 512:(j + 1) * 512] = _sigmoid(proj(c0 + j * 512, 512)).astype(BF16)


def _inproj(x2, gpre, w_in):
    T = x2.shape[0]
    d_in = w_in.shape[1]
    row = lambda i: (i, 0)
    fixed = lambda i: (0, 0)
    tr_shape = (T // TK, N_HEADS, V_DIM, TK)
    tr_spec = pl.BlockSpec((TM_IN // TK, N_HEADS, V_DIM, TK), lambda i: (i, 0, 0, 0))
    rows = lambda w: (pl.BlockSpec((TM_IN, w), row), jax.ShapeDtypeStruct((T, w), BF16))
    tr = (tr_spec, jax.ShapeDtypeStruct(tr_shape, BF16))
    outs = (rows(D_CONV), rows(D_CONV), tr, rows(D_QK), tr, rows(D_ATTN), rows(2 * D_MODEL))
    return pl.pallas_call(
        _inproj_kernel,
        grid=(T // TM_IN,),
        in_specs=[
            pl.BlockSpec((TM_IN, D_MODEL), row),
            pl.BlockSpec((1, D_MODEL), fixed),
            pl.BlockSpec((D_MODEL, d_in), fixed, pipeline_mode=pl.Buffered(1)),
        ],
        out_specs=[o[0] for o in outs],
        out_shape=[o[1] for o in outs],
        compiler_params=pltpu.CompilerParams(
            dimension_semantics=("arbitrary",), vmem_limit_bytes=VMEM_LIMIT_BYTES),
        name="inproj",
    )(x2, gpre, w_in)


def _conv_kernel(glu_ref, halo_ref, zc_ref, w_ref, b_ref, lng_ref, lnb_ref, o_ref, xpad_ref, acc_ref):
    i = pl.program_id(1)
    halo = jnp.where(i == 0, 0.0, halo_ref[0].astype(F32))
    cur = glu_ref[0].astype(F32)
    for c in range(D_CONV // 128):
        xpad_ref[c, 0:CONV_HALO, :] = halo[:, c * 128:(c + 1) * 128]
        xpad_ref[c, CONV_HALO:, :] = cur[:, c * 128:(c + 1) * 128]

    shift = CONV_HALO - (CONV_WIDTH - 1)

    def rows(r, carry):
        r0 = pl.multiple_of(r * CONV_ROWS, CONV_ROWS)
        for c in range(D_CONV // 128):
            acc = jnp.zeros((CONV_ROWS, 128), F32)
            for j in range(CONV_WIDTH):
                xs = xpad_ref[c, pl.ds(r0 + (shift + j), CONV_ROWS), :]
                acc = acc + xs * w_ref[j:j + 1, c * 128:(c + 1) * 128]
            acc_ref[pl.ds(r0, CONV_ROWS), c * 128:(c + 1) * 128] = acc
        return carry

    lax.fori_loop(0, TS_CONV // CONV_ROWS, rows, 0)

    y = acc_ref[...] + b_ref[...]
    mu = jnp.mean(y, axis=-1, keepdims=True)
    yc = y - mu
    var = jnp.mean(yc * yc, axis=-1, keepdims=True)
    yn = yc * lax.rsqrt(var + EPS) * lng_ref[...] + lnb_ref[...]
    o_ref[0] = (_silu(yn) * zc_ref[0].astype(F32)).astype(BF16)


def _conv(glu, zc, conv_w, conv_b, ln_g, ln_b):
    B, S, _ = glu.shape
    blocks_per_halo = TS_CONV // CONV_HALO
    cur = lambda b, i: (b, i, 0)
    prev = lambda b, i: (b, jnp.maximum(i * blocks_per_halo - 1, 0), 0)
    fixed = lambda b, i: (0, 0)
    return pl.pallas_call(
        _conv_kernel,
        grid=(B, S // TS_CONV),
        in_specs=[
            pl.BlockSpec((1, TS_CONV, D_CONV), cur),
            pl.BlockSpec((1, CONV_HALO, D_CONV), prev),
            pl.BlockSpec((1, TS_CONV, D_CONV), cur),
            pl.BlockSpec((CONV_WIDTH, D_CONV), fixed),
            pl.BlockSpec((1, D_CONV), fixed),
            pl.BlockSpec((1, D_CONV), fixed),
            pl.BlockSpec((1, D_CONV), fixed),
        ],
        out_specs=pl.BlockSpec((1, TS_CONV, D_CONV), cur),
        out_shape=jax.ShapeDtypeStruct((B, S, D_CONV), BF16),
        scratch_shapes=[
            pltpu.VMEM((D_CONV // 128, TS_CONV + CONV_HALO, 128), F32),
            pltpu.VMEM((TS_CONV, D_CONV), F32),
        ],
        compiler_params=pltpu.CompilerParams(
            dimension_semantics=("arbitrary", "arbitrary"), vmem_limit_bytes=VMEM_LIMIT_BYTES),
        name="conv",
    )(glu, glu, zc, conv_w, conv_b, ln_g, ln_b)


def _bf16_round(x):
    bits = struct.unpack("<I", struct.pack("<f", x))[0]
    bits = (bits + 0x7FFF + ((bits >> 16) & 1)) & 0xFFFF0000
    return struct.unpack("<f", struct.pack("<I", bits))[0]


def _bf16_split3(c):
    c1 = _bf16_round(c)
    c2 = _bf16_round(c - c1)
    c3 = _bf16_round(c - c1 - c2)
    return c1, c2, c3


def _attn_kernel(qt_ref, k_ref, vt_ref, za_ref, lq1_ref, lk1_ref, lq2_ref, lk2_ref, subg_ref,
                 o_ref, qaug_ref, e_ref, tile_ref, acc_ref, m_ref, alpha_ref, s_ref, p_ref):
    i = pl.program_id(1)
    t0 = i * TQ
    slopes2 = [s * LOG2E for s in SLOPES]

    @pl.when((pl.program_id(0) == 0) & (i == 0))
    def _():
        sl = lax.broadcasted_iota(jnp.int32, (TK, 2 * TQ), 0)
        tl = lax.broadcasted_iota(jnp.int32, (TK, 2 * TQ), 1) % TQ
        allowed = (sl // CHUNK) <= (tl // CHUNK)
        rel = (tl - jnp.abs(tl - sl) - sl).astype(F32)
        crow = lax.broadcasted_iota(jnp.int32, (2 * HEAD_DIM, 2 * TQ), 0)
        for h in range(N_HEADS):
            tile_ref[h] = jnp.where(allowed, slopes2[h] * rel, -jnp.inf)
            c1, c2, c3 = _bf16_split3(slopes2[h])
            consts = jnp.where(crow == 0, c1, jnp.where(crow == 1, c2, jnp.where(crow == 2, c3, 0.0)))
            qaug_ref[h, 2 * HEAD_DIM:, :] = consts.astype(BF16)
        lane = lax.broadcasted_iota(jnp.int32, (TK, 2 * HEAD_DIM), 1)
        row = lax.broadcasted_iota(jnp.int32, (TK, 2 * HEAD_DIM), 0)
        e_ref[...] = jnp.where(lane < 3, row, 0).astype(F32).astype(BF16)

    zeros = jnp.zeros((HEAD_DIM, TQ), BF16)
    for h in range(N_HEADS):
        qt = qt_ref[0, h]
        qaug_ref[h, 0:2 * HEAD_DIM, :] = jnp.concatenate(
            [jnp.concatenate([qt[:HEAD_DIM], zeros], axis=0),
             jnp.concatenate([zeros, qt[HEAD_DIM:]], axis=0)], axis=1)

    ones = jnp.ones((16, TK), BF16)

    def scores(h, j):
        j0 = pl.multiple_of(j * TK, TK)
        lhs = jnp.concatenate([k_ref[0, pl.ds(j0, TK), h * V_DIM:(h + 1) * V_DIM], e_ref[...]], axis=1)
        return jnp.dot(lhs, qaug_ref[h], preferred_element_type=F32)

    def softmax(h, j, first):
        s = s_ref[h]
        if first:
            s = s + tile_ref[h]
            m_new = jnp.max(s, axis=0, keepdims=True)
            base = m_new
            alpha_ref[h] = jnp.zeros((1, 2 * TQ), F32)
        else:
            shift = slopes2[h] * jnp.full((1, 2 * TQ), j * TK - t0, jnp.int32).astype(F32)
            m_old = m_ref[h]
            m_new = jnp.maximum(m_old, jnp.max(s, axis=0, keepdims=True) + shift)
            base = m_new - shift
            alpha_ref[h] = jnp.exp2(m_old - m_new)
        p_ref[h] = jnp.exp2(s - base).astype(BF16)
        m_ref[h] = m_new

    def accumulate(h, j):
        lhs_v = jnp.concatenate([vt_ref[j, h], ones], axis=0)
        pv = jnp.dot(lhs_v, p_ref[h], preferred_element_type=F32)
        acc_ref[h] = acc_ref[h] * alpha_ref[h] + pv

    for h in range(N_HEADS):
        acc_ref[h] = jnp.zeros((V_DIM + 16, 2 * TQ), F32)
        s_ref[h] = scores(h, i)
    for h in range(N_HEADS):
        softmax(h, i, True)
        s_ref[h] = scores(h, 0)

    def body(t, carry):
        prev = jnp.where(t == 0, i, t - 1)
        nxt = jnp.minimum(t + 1, i)
        for h in range(N_HEADS):
            accumulate(h, prev)
        for h in range(N_HEADS):
            s_nxt = scores(h, nxt)
            softmax(h, t, False)
            s_ref[h] = s_nxt
        return carry

    lax.fori_loop(0, i, body, 0)
    for h in range(N_HEADS):
        accumulate(h, jnp.where(i == 0, i, i - 1))

    lam = (jnp.exp(jnp.sum(lq1_ref[...] * lk1_ref[...], axis=-1, keepdims=True))
           - jnp.exp(jnp.sum(lq2_ref[...] * lk2_ref[...], axis=-1, keepdims=True))
           + LAMBDA_INIT)
    for h in range(N_HEADS):
        hs = slice(h * V_DIM, (h + 1) * V_DIM)
        a = acc_ref[h]
        r = 1.0 / a[V_DIM:V_DIM + 1, :]
        o = (a[:V_DIM, :TQ] * r[:, :TQ] - lam * (a[:V_DIM, TQ:] * r[:, TQ:])).T
        o = o * lax.rsqrt(jnp.mean(o * o, axis=-1, keepdims=True) + EPS)
        o = o * subg_ref[...] * (1.0 - LAMBDA_INIT)
        o_ref[0, :, hs] = (o * za_ref[0, :, hs].astype(F32)).astype(BF16)


def _attn(qt, k, vt, za, lq1, lk1, lq2, lk2, subg):
    B, S, _ = k.shape
    nq = S // TQ
    blk = lambda b, i: (b, i, 0)
    whole = lambda b, i: (b, 0, 0)
    fixed = lambda b, i: (0, 0)
    return pl.pallas_call(
        _attn_kernel,
        grid=(B, nq),
        in_specs=[
            pl.BlockSpec((1, N_HEADS, 2 * HEAD_DIM, TQ), lambda b, i: (b * nq + i, 0, 0, 0)),
            pl.BlockSpec((1, S, D_QK), whole),
            pl.BlockSpec((S // TK, N_HEADS, V_DIM, TK), lambda b, i: (b, 0, 0, 0)),
            pl.BlockSpec((1, TQ, D_ATTN), blk),
            pl.BlockSpec((1, HEAD_DIM), fixed),
            pl.BlockSpec((1, HEAD_DIM), fixed),
            pl.BlockSpec((1, HEAD_DIM), fixed),
            pl.BlockSpec((1, HEAD_DIM), fixed),
            pl.BlockSpec((1, V_DIM), fixed),
        ],
        out_specs=pl.BlockSpec((1, TQ, D_ATTN), blk),
        out_shape=jax.ShapeDtypeStruct((B, S, D_ATTN), BF16),
        scratch_shapes=[
            pltpu.VMEM((N_HEADS, 4 * HEAD_DIM, 2 * TQ), BF16),
            pltpu.VMEM((TK, 2 * HEAD_DIM), BF16),
            pltpu.VMEM((N_HEADS, TK, 2 * TQ), F32),
            pltpu.VMEM((N_HEADS, V_DIM + 16, 2 * TQ), F32),
            pltpu.VMEM((N_HEADS, 1, 2 * TQ), F32),
            pltpu.VMEM((N_HEADS, 1, 2 * TQ), F32),
            pltpu.VMEM((N_HEADS, TK, 2 * TQ), F32),
            pltpu.VMEM((N_HEADS, TK, 2 * TQ), BF16),
        ],
        compiler_params=pltpu.CompilerParams(
            dimension_semantics=("arbitrary", "arbitrary"), vmem_limit_bytes=VMEM_LIMIT_BYTES),
        name="attn",
    )(qt, k, vt, za, lq1, lk1, lq2, lk2, subg)


def _out_kernel(x_ref, hc_ref, ha_ref, g_ref, wc_ref, wa_ref, wo_ref, gpost_ref, o_ref):
    yc = jnp.dot(hc_ref[...], wc_ref[...], preferred_element_type=F32)
    ya = jnp.dot(ha_ref[...], wa_ref[...], preferred_element_type=F32)
    mixed = (g_ref[:, :D_MODEL].astype(F32) * yc + g_ref[:, D_MODEL:].astype(F32) * ya).astype(BF16)
    out = jnp.dot(mixed, wo_ref[...], preferred_element_type=F32)
    ms = jnp.mean(out * out, axis=-1, keepdims=True)
    o_ref[...] = x_ref[...] + out * lax.rsqrt(ms + EPS) * gpost_ref[...]


def _out(x2, hc, ha, g, wc, wa, wo, gpost):
    T = x2.shape[0]
    row = lambda i: (i, 0)
    fixed = lambda i: (0, 0)
    return pl.pallas_call(
        _out_kernel,
        grid=(T // TM_OUT,),
        in_specs=[
            pl.BlockSpec((TM_OUT, D_MODEL), row),
            pl.BlockSpec((TM_OUT, D_CONV), row),
            pl.BlockSpec((TM_OUT, D_ATTN), row),
            pl.BlockSpec((TM_OUT, 2 * D_MODEL), row),
            pl.BlockSpec((D_CONV, D_MODEL), fixed),
            pl.BlockSpec((D_ATTN, D_MODEL), fixed),
            pl.BlockSpec((D_MODEL, D_MODEL), fixed),
            pl.BlockSpec((1, D_MODEL), fixed),
        ],
        out_specs=pl.BlockSpec((TM_OUT, D_MODEL), row),
        out_shape=jax.ShapeDtypeStruct((T, D_MODEL), F32),
        compiler_params=pltpu.CompilerParams(
            dimension_semantics=("arbitrary",), vmem_limit_bytes=VMEM_LIMIT_BYTES),
        name="out",
    )(x2, hc, ha, g, wc, wa, wo, gpost)


def kernel(x, w_in, conv_w, conv_b, conv_ln_g, conv_ln_b, w_conv_proj, lambda_q1, lambda_k1,
           lambda_q2, lambda_k2, subln_g, w_attn_proj, w_out, norm_pre_g, norm_post_g):
    B, S, D = x.shape
    assert (D, w_in.shape[0]) == (D_MODEL, 1) and TQ == TK
    assert S % max(TQ, TS_CONV) == 0 and (B * S) % TM_IN == 0
    T = B * S
    x2 = x.reshape(T, D)
    glu, zc, qt, k, vt, za, g = _inproj(x2, norm_pre_g, w_in[0].astype(BF16))
    seq = lambda a: a.reshape(B, S, a.shape[-1])
    hc = _conv(seq(glu), seq(zc), conv_w[0], conv_b, conv_ln_g, conv_ln_b)
    ha = _attn(qt, seq(k), vt, seq(za), lambda_q1, lambda_k1, lambda_q2, lambda_k2, subln_g)
    y = _out(x2, hc.reshape(T, D_CONV), ha.reshape(T, D_ATTN), g,
             w_conv_proj[0].astype(BF16), w_attn_proj[0].astype(BF16), w_out[0].astype(BF16),
             norm_post_g)
    return y.reshape(B, S, D)
```

```python
import math
import struct

import jax
import jax.numpy as jnp
from jax import lax
from jax.experimental import pallas as pl
from jax.experimental.pallas import tpu as pltpu

D_MODEL = 1024
CHUNK = 64
CONV_WIDTH = 31
D_CONV = 512
N_HEADS = 4
HEAD_DIM = 64
V_DIM = 128
D_ATTN = 512
D_QK = 512
EPS = 1e-6
LAMBDA_INIT = 0.8 - 0.6 * math.exp(-0.3 * 0)
LOG2E = math.log2(math.e)
SLOPES = tuple(2.0 ** (-8.0 * (h + 1) / N_HEADS) for h in range(N_HEADS))

BF16 = jnp.bfloat16
F32 = jnp.float32

VMEM_LIMIT_BYTES = 56 * 1024 * 1024

TM_IN = 1024
TS_CONV = 256
CONV_HALO = 32
CONV_ROWS = 64
TQ = 256
TK = 256
TM_OUT = 1024


def _sigmoid(x):
    return 1.0 / (1.0 + jnp.exp(-x))


def _silu(x):
    return x * _sigmoid(x)


def _inproj_kernel(x_ref, gpre_ref, w_ref, glu_ref, zc_ref, qt_ref, k_ref, vt_ref, za_ref, g_ref):
    x = x_ref[...]
    ms = jnp.mean(x * x, axis=-1, keepdims=True)
    h = (x * lax.rsqrt(ms + EPS) * gpre_ref[...]).astype(BF16)

    def proj(c0, width):
        return jnp.dot(h, w_ref[:, c0:c0 + width], preferred_element_type=F32)

    def store_transposed(t_ref, y):
        for blk in range(TM_IN // TK):
            for hd in range(N_HEADS):
                t_ref[blk, hd] = y[blk * TK:(blk + 1) * TK, hd * V_DIM:(hd + 1) * V_DIM].T.astype(BF16)

    a = proj(0, D_CONV)
    b = proj(D_CONV, D_CONV)
    glu_ref[...] = (a * _sigmoid(b)).astype(BF16)
    zc_ref[...] = _silu(proj(2 * D_CONV, D_CONV)).astype(BF16)
    c0 = 3 * D_CONV
    store_transposed(qt_ref, proj(c0, D_QK) * (LOG2E / math.sqrt(HEAD_DIM)))
    k_ref[...] = proj(c0 + D_QK, D_QK).astype(BF16)
    store_transposed(vt_ref, proj(c0 + 2 * D_QK, D_ATTN))
    za_ref[...] = _silu(proj(c0 + 2 * D_QK + D_ATTN, D_ATTN)).astype(BF16)
    c0 = c0 + 2 * D_QK + 2 * D_ATTN
    for j in range(2 * D_MODEL // 512):
        g_ref[:, j * 512:(j + 1) * 512] = _sigmoid(proj(c0 + j * 512, 512)).astype(BF16)


def _inproj(x2, gpre, w_in):
    T = x2.shape[0]
    d_in = w_in.shape[1]
    row = lambda i: (i, 0)
    fixed = lambda i: (0, 0)
    tr_shape = (T // TK, N_HEADS, V_DIM, TK)
    tr_spec = pl.BlockSpec((TM_IN // TK, N_HEADS, V_DIM, TK), lambda i: (i, 0, 0, 0))
    rows = lambda w: (pl.BlockSpec((TM_IN, w), row), jax.ShapeDtypeStruct((T, w), BF16))
    tr = (tr_spec, jax.ShapeDtypeStruct(tr_shape, BF16))
    outs = (rows(D_CONV), rows(D_CONV), tr, rows(D_QK), tr, rows(D_ATTN), rows(2 * D_MODEL))
    return pl.pallas_call(
        _inproj_kernel,
        grid=(T // TM_IN,),
        in_specs=[
            pl.BlockSpec((TM_IN, D_MODEL), row),
            pl.BlockSpec((1, D_MODEL), fixed),
            pl.BlockSpec((D_MODEL, d_in), fixed, pipeline_mode=pl.Buffered(1)),
        ],
        out_specs=[o[0] for o in outs],
        out_shape=[o[1] for o in outs],
        compiler_params=pltpu.CompilerParams(
            dimension_semantics=("arbitrary",), vmem_limit_bytes=VMEM_LIMIT_BYTES),
        name="inproj",
    )(x2, gpre, w_in)


def _conv_kernel(glu_ref, halo_ref, zc_ref, w_ref, b_ref, lng_ref, lnb_ref, o_ref, xpad_ref, acc_ref):
    i = pl.program_id(1)
    halo = jnp.where(i == 0, 0.0, halo_ref[0].astype(F32))
    cur = glu_ref[0].astype(F32)
    for c in range(D_CONV // 128):
        xpad_ref[c, 0:CONV_HALO, :] = halo[:, c * 128:(c + 1) * 128]
        xpad_ref[c, CONV_HALO:, :] = cur[:, c * 128:(c + 1) * 128]

    shift = CONV_HALO - (CONV_WIDTH - 1)

    def rows(r, carry):
        r0 = pl.multiple_of(r * CONV_ROWS, CONV_ROWS)
        for c in range(D_CONV // 128):
            acc = jnp.zeros((CONV_ROWS, 128), F32)
            for j in range(CONV_WIDTH):
                xs = xpad_ref[c, pl.ds(r0 + (shift + j), CONV_ROWS), :]
                acc = acc + xs * w_ref[j:j + 1, c * 128:(c + 1) * 128]
            acc_ref[pl.ds(r0, CONV_ROWS), c * 128:(c + 1) * 128] = acc
        return carry

    lax.fori_loop(0, TS_CONV // CONV_ROWS, rows, 0)

    y = acc_ref[...] + b_ref[...]
    mu = jnp.mean(y, axis=-1, keepdims=True)
    yc = y - mu
    var = jnp.mean(yc * yc, axis=-1, keepdims=True)
    yn = yc * lax.rsqrt(var + EPS) * lng_ref[...] + lnb_ref[...]
    o_ref[0] = (_silu(yn) * zc_ref[0].astype(F32)).astype(BF16)


def _conv(glu, zc, conv_w, conv_b, ln_g, ln_b):
    B, S, _ = glu.shape
    blocks_per_halo = TS_CONV // CONV_HALO
    cur = lambda b, i: (b, i, 0)
    prev = lambda b, i: (b, jnp.maximum(i * blocks_per_halo - 1, 0), 0)
    fixed = lambda b, i: (0, 0)
    return pl.pallas_call(
        _conv_kernel,
        grid=(B, S // TS_CONV),
        in_specs=[
            pl.BlockSpec((1, TS_CONV, D_CONV), cur),
            pl.BlockSpec((1, CONV_HALO, D_CONV), prev),
            pl.BlockSpec((1, TS_CONV, D_CONV), cur),
            pl.BlockSpec((CONV_WIDTH, D_CONV), fixed),
            pl.BlockSpec((1, D_CONV), fixed),
            pl.BlockSpec((1, D_CONV), fixed),
            pl.BlockSpec((1, D_CONV), fixed),
        ],
        out_specs=pl.BlockSpec((1, TS_CONV, D_CONV), cur),
        out_shape=jax.ShapeDtypeStruct((B, S, D_CONV), BF16),
        scratch_shapes=[
            pltpu.VMEM((D_CONV // 128, TS_CONV + CONV_HALO, 128), F32),
            pltpu.VMEM((TS_CONV, D_CONV), F32),
        ],
        compiler_params=pltpu.CompilerParams(
            dimension_semantics=("arbitrary", "arbitrary"), vmem_limit_bytes=VMEM_LIMIT_BYTES),
        name="conv",
    )(glu, glu, zc, conv_w, conv_b, ln_g, ln_b)


def _bf16_round(x):
    bits = struct.unpack("<I", struct.pack("<f", x))[0]
    bits = (bits + 0x7FFF + ((bits >> 16) & 1)) & 0xFFFF0000
    return struct.unpack("<f", struct.pack("<I", bits))[0]


def _bf16_split3(c):
    c1 = _bf16_round(c)
    c2 = _bf16_round(c - c1)
    c3 = _bf16_round(c - c1 - c2)
    return c1, c2, c3


def _attn_kernel(qt_ref, k_ref, vt_ref, za_ref, lq1_ref, lk1_ref, lq2_ref, lk2_ref, subg_ref,
                 o_ref, qaug_ref, e_ref, tile_ref, acc_ref, m_ref, alpha_ref, s_ref, p_ref):
    i = pl.program_id(1)
    t0 = i * TQ
    slopes2 = [s * LOG2E for s in SLOPES]

    @pl.when((pl.program_id(0) == 0) & (i == 0))
    def _():
        sl = lax.broadcasted_iota(jnp.int32, (TK, 2 * TQ), 0)
        tl = lax.broadcasted_iota(jnp.int32, (TK, 2 * TQ), 1) % TQ
        allowed = (sl // CHUNK) <= (tl // CHUNK)
        rel = (tl - jnp.abs(tl - sl) - sl).astype(F32)
        crow = lax.broadcasted_iota(jnp.int32, (2 * HEAD_DIM, 2 * TQ), 0)
        for h in range(N_HEADS):
            tile_ref[h] = jnp.where(allowed, slopes2[h] * rel, -jnp.inf)
            c1, c2, c3 = _bf16_split3(slopes2[h])
            consts = jnp.where(crow == 0, c1, jnp.where(crow == 1, c2, jnp.where(crow == 2, c3, 0.0)))
            qaug_ref[h, 2 * HEAD_DIM:, :] = consts.astype(BF16)
        lane = lax.broadcasted_iota(jnp.int32, (TK, 2 * HEAD_DIM), 1)
        row = lax.broadcasted_iota(jnp.int32, (TK, 2 * HEAD_DIM), 0)
        e_ref[...] = jnp.where(lane < 3, row, 0).astype(F32).astype(BF16)

    zeros = jnp.zeros((HEAD_DIM, TQ), BF16)
    for h in range(N_HEADS):
        qt = qt_ref[0, h]
        qaug_ref[h, 0:2 * HEAD_DIM, :] = jnp.concatenate(
            [jnp.concatenate([qt[:HEAD_DIM], zeros], axis=0),
             jnp.concatenate([zeros, qt[HEAD_DIM:]], axis=0)], axis=1)

    ones = jnp.ones((16, TK), BF16)

    def scores(h, j):
        j0 = pl.multiple_of(j * TK, TK)
        lhs = jnp.concatenate([k_ref[0, pl.ds(j0, TK), h * V_DIM:(h + 1) * V_DIM], e_ref[...]], axis=1)
        return jnp.dot(lhs, qaug_ref[h], preferred_element_type=F32)

    def softmax(h, j, first):
        s = s_ref[h]
        if first:
            s = s + tile_ref[h]
            m_new = jnp.max(s, axis=0, keepdims=True)
            base = m_new
            alpha_ref[h] = jnp.zeros((1, 2 * TQ), F32)
        else:
            shift = slopes2[h] * jnp.full((1, 2 * TQ), j * TK - t0, jnp.int32).astype(F32)
            m_old = m_ref[h]
            m_new = jnp.maximum(m_old, jnp.max(s, axis=0, keepdims=True) + shift)
            base = m_new - shift
            alpha_ref[h] = jnp.exp2(m_old - m_new)
        p_ref[h] = jnp.exp2(s - base).astype(BF16)
        m_ref[h] = m_new

    def accumulate(h, j):
        lhs_v = jnp.concatenate([vt_ref[j, h], ones], axis=0)
        pv = jnp.dot(lhs_v, p_ref[h], preferred_element_type=F32)
        acc_ref[h] = acc_ref[h] * alpha_ref[h] + pv

    for h in range(N_HEADS):
        acc_ref[h] = jnp.zeros((V_DIM + 16, 2 * TQ), F32)
        s_ref[h] = scores(h, i)
    for h in range(N_HEADS):
        softmax(h, i, True)
        s_ref[h] = scores(h, 0)

    def body(t, carry):
        prev = jnp.where(t == 0, i, t - 1)
        nxt = jnp.minimum(t + 1, i)
        for h in range(N_HEADS):
            s_nxt = scores(h, nxt)
            accumulate(h, prev)
            softmax(h, t, False)
            s_ref[h] = s_nxt
        return carry

    lax.fori_loop(0, i, body, 0)
    for h in range(N_HEADS):
        accumulate(h, jnp.where(i == 0, i, i - 1))

    lam = (jnp.exp(jnp.sum(lq1_ref[...] * lk1_ref[...], axis=-1, keepdims=True))
           - jnp.exp(jnp.sum(lq2_ref[...] * lk2_ref[...], axis=-1, keepdims=True))
           + LAMBDA_INIT)
    for h in range(N_HEADS):
        hs = slice(h * V_DIM, (h + 1) * V_DIM)
        a = acc_ref[h]
        r = 1.0 / a[V_DIM:V_DIM + 1, :]
        o = (a[:V_DIM, :TQ] * r[:, :TQ] - lam * (a[:V_DIM, TQ:] * r[:, TQ:])).T
        o = o * lax.rsqrt(jnp.mean(o * o, axis=-1, keepdims=True) + EPS)
        o = o * subg_ref[...] * (1.0 - LAMBDA_INIT)
        o_ref[0, :, hs] = (o * za_ref[0, :, hs].astype(F32)).astype(BF16)


def _attn(qt, k, vt, za, lq1, lk1, lq2, lk2, subg):
    B, S, _ = k.shape
    nq = S // TQ
    blk = lambda b, i: (b, i, 0)
    whole = lambda b, i: (b, 0, 0)
    fixed = lambda b, i: (0, 0)
    return pl.pallas_call(
        _attn_kernel,
        grid=(B, nq),
        in_specs=[
            pl.BlockSpec((1, N_HEADS, 2 * HEAD_DIM, TQ), lambda b, i: (b * nq + i, 0, 0, 0)),
            pl.BlockSpec((1, S, D_QK), whole),
            pl.BlockSpec((S // TK, N_HEADS, V_DIM, TK), lambda b, i: (b, 0, 0, 0)),
            pl.BlockSpec((1, TQ, D_ATTN), blk),
            pl.BlockSpec((1, HEAD_DIM), fixed),
            pl.BlockSpec((1, HEAD_DIM), fixed),
            pl.BlockSpec((1, HEAD_DIM), fixed),
            pl.BlockSpec((1, HEAD_DIM), fixed),
            pl.BlockSpec((1, V_DIM), fixed),
        ],
        out_specs=pl.BlockSpec((1, TQ, D_ATTN), blk),
        out_shape=jax.ShapeDtypeStruct((B, S, D_ATTN), BF16),
        scratch_shapes=[
            pltpu.VMEM((N_HEADS, 4 * HEAD_DIM, 2 * TQ), BF16),
            pltpu.VMEM((TK, 2 * HEAD_DIM), BF16),
            pltpu.VMEM((N_HEADS, TK, 2 * TQ), F32),
            pltpu.VMEM((N_HEADS, V_DIM + 16, 2 * TQ), F32),
            pltpu.VMEM((N_HEADS, 1, 2 * TQ), F32),
            pltpu.VMEM((N_HEADS, 1, 2 * TQ), F32),
            pltpu.VMEM((N_HEADS, TK, 2 * TQ), F32),
            pltpu.VMEM((N_HEADS, TK, 2 * TQ), BF16),
        ],
        compiler_params=pltpu.CompilerParams(
            dimension_semantics=("arbitrary", "arbitrary"), vmem_limit_bytes=VMEM_LIMIT_BYTES),
        name="attn",
    )(qt, k, vt, za, lq1, lk1, lq2, lk2, subg)


def _out_kernel(x_ref, hc_ref, ha_ref, g_ref, wc_ref, wa_ref, wo_ref, gpost_ref, o_ref):
    yc = jnp.dot(hc_ref[...], wc_ref[...], preferred_element_type=F32)
    ya = jnp.dot(ha_ref[...], wa_ref[...], preferred_element_type=F32)
    mixed = (g_ref[:, :D_MODEL].astype(F32) * yc + g_ref[:, D_MODEL:].astype(F32) * ya).astype(BF16)
    out = jnp.dot(mixed, wo_ref[...], preferred_element_type=F32)
    ms = jnp.mean(out * out, axis=-1, keepdims=True)
    o_ref[...] = x_ref[...] + out * lax.rsqrt(ms + EPS) * gpost_ref[...]


def _out(x2, hc, ha, g, wc, wa, wo, gpost):
    T = x2.shape[0]
    row = lambda i: (i, 0)
    fixed = lambda i: (0, 0)
    return pl.pallas_call(
        _out_kernel,
        grid=(T // TM_OUT,),
        in_specs=[
            pl.BlockSpec((TM_OUT, D_MODEL), row),
            pl.BlockSpec((TM_OUT, D_CONV), row),
            pl.BlockSpec((TM_OUT, D_ATTN), row),
            pl.BlockSpec((TM_OUT, 2 * D_MODEL), row),
            pl.BlockSpec((D_CONV, D_MODEL), fixed),
            pl.BlockSpec((D_ATTN, D_MODEL), fixed),
            pl.BlockSpec((D_MODEL, D_MODEL), fixed),
            pl.BlockSpec((1, D_MODEL), fixed),
        ],
        out_specs=pl.BlockSpec((TM_OUT, D_MODEL), row),
        out_shape=jax.ShapeDtypeStruct((T, D_MODEL), F32),
        compiler_params=pltpu.CompilerParams(
            dimension_semantics=("arbitrary",), vmem_limit_bytes=VMEM_LIMIT_BYTES),
        name="out",
    )(x2, hc, ha, g, wc, wa, wo, gpost)


def kernel(x, w_in, conv_w, conv_b, conv_ln_g, conv_ln_b, w_conv_proj, lambda_q1, lambda_k1,
           lambda_q2, lambda_k2, subln_g, w_attn_proj, w_out, norm_pre_g, norm_post_g):
    B, S, D = x.shape
    assert (D, w_in.shape[0]) == (D_MODEL, 1) and TQ == TK
    assert S % max(TQ, TS_CONV) == 0 and (B * S) % TM_IN == 0
    T = B * S
    x2 = x.reshape(T, D)
    glu, zc, qt, k, vt, za, g = _inproj(x2, norm_pre_g, w_in[0].astype(BF16))
    seq = lambda a: a.reshape(B, S, a.shape[-1])
    hc = _conv(seq(glu), seq(zc), conv_w[0], conv_b, conv_ln_g, conv_ln_b)
    ha = _attn(qt, seq(k), vt, seq(za), lambda_q1, lambda_k1, lambda_q2, lambda_k2, subln_g)
    y = _out(x2, hc.reshape(T, D_CONV), ha.reshape(T, D_ATTN), g,
             w_conv_proj[0].astype(BF16), w_attn_proj[0].astype(BF16), w_out[0].astype(BF16),
             norm_post_g)
    return y.reshape(B, S, D)
```

```python
import math
import struct

import jax
import jax.numpy as jnp
from jax import lax
from jax.experimental import pallas as pl
from jax.experimental.pallas import tpu as pltpu

D_MODEL = 1024
CHUNK = 64
CONV_WIDTH = 31
D_CONV = 512
N_HEADS = 4
HEAD_DIM = 64
V_DIM = 128
D_ATTN = 512
D_QK = 512
EPS = 1e-6
LAMBDA_INIT = 0.8 - 0.6 * math.exp(-0.3 * 0)
LOG2E = math.log2(math.e)
SLOPES = tuple(2.0 ** (-8.0 * (h + 1) / N_HEADS) for h in range(N_HEADS))

BF16 = jnp.bfloat16
F32 = jnp.float32

VMEM_LIMIT_BYTES = 56 * 1024 * 1024

TM_IN = 1024
CONV_HALO = 32
CONV_ROWS = 32
TQ = 256
TK = 256
TM_OUT = 1024


def _sigmoid(x):
    return 1.0 / (1.0 + jnp.exp(-x))


def _silu(x):
    return x * _sigmoid(x)


def _inproj_kernel(x_ref, gpre_ref, w_ref, glu_ref, zc_ref, qt_ref, k_ref, vt_ref, za_ref, g_ref):
    x = x_ref[...]
    ms = jnp.mean(x * x, axis=-1, keepdims=True)
    h = (x * lax.rsqrt(ms + EPS) * gpre_ref[...]).astype(BF16)

    def proj(c0, width):
        return jnp.dot(h, w_ref[:, c0:c0 + width], preferred_element_type=F32)

    def store_transposed(t_ref, y):
        for blk in range(TM_IN // TK):
            for hd in range(N_HEADS):
                t_ref[blk, hd] = y[blk * TK:(blk + 1) * TK, hd * V_DIM:(hd + 1) * V_DIM].T.astype(BF16)

    a = proj(0, D_CONV)
    b = proj(D_CONV, D_CONV)
    glu_ref[...] = (a * _sigmoid(b)).astype(BF16)
    zc_ref[...] = _silu(proj(2 * D_CONV, D_CONV)).astype(BF16)
    c0 = 3 * D_CONV
    store_transposed(qt_ref, proj(c0, D_QK) * (LOG2E / math.sqrt(HEAD_DIM)))
    k_ref[...] = proj(c0 + D_QK, D_QK).astype(BF16)
    store_transposed(vt_ref, proj(c0 + 2 * D_QK, D_ATTN))
    za_ref[...] = _silu(proj(c0 + 2 * D_QK + D_ATTN, D_ATTN)).astype(BF16)
    c0 = c0 + 2 * D_QK + 2 * D_ATTN
    for j in range(2 * D_MODEL // 512):
        g_ref[:, j * 512:(j + 1) * 512] = _sigmoid(proj(c0 + j * 512, 512)).astype(BF16)


def _inproj(x2, gpre, w_in):
    T = x2.shape[0]
    d_in = w_in.shape[1]
    row = lambda i: (i, 0)
    fixed = lambda i: (0, 0)
    tr_shape = (T // TK, N_HEADS, V_DIM, TK)
    tr_spec = pl.BlockSpec((TM_IN // TK, N_HEADS, V_DIM, TK), lambda i: (i, 0, 0, 0))
    rows = lambda w: (pl.BlockSpec((TM_IN, w), row), jax.ShapeDtypeStruct((T, w), BF16))
    tr = (tr_spec, jax.ShapeDtypeStruct(tr_shape, BF16))
    outs = (rows(D_CONV), rows(D_CONV), tr, rows(D_QK), tr, rows(D_ATTN), rows(2 * D_MODEL))
    return pl.pallas_call(
        _inproj_kernel,
        grid=(T // TM_IN,),
        in_specs=[
            pl.BlockSpec((TM_IN, D_MODEL), row),
            pl.BlockSpec((1, D_MODEL), fixed),
            pl.BlockSpec((D_MODEL, d_in), fixed, pipeline_mode=pl.Buffered(1)),
        ],
        out_specs=[o[0] for o in outs],
        out_shape=[o[1] for o in outs],
        compiler_params=pltpu.CompilerParams(
            dimension_semantics=("arbitrary",), vmem_limit_bytes=VMEM_LIMIT_BYTES),
        name="inproj",
    )(x2, gpre, w_in)


def _bf16_round(x):
    bits = struct.unpack("<I", struct.pack("<f", x))[0]
    bits = (bits + 0x7FFF + ((bits >> 16) & 1)) & 0xFFFF0000
    return struct.unpack("<f", struct.pack("<I", bits))[0]


def _bf16_split3(c):
    c1 = _bf16_round(c)
    c2 = _bf16_round(c - c1)
    c3 = _bf16_round(c - c1 - c2)
    return c1, c2, c3


def _attn_kernel(qt_ref, k_ref, vt_ref, za_ref, lq1_ref, lk1_ref, lq2_ref, lk2_ref, subg_ref,
                 glu_ref, cw_ref, cb_ref, lng_ref, lnb_ref,
                 o_ref, hc_ref, qaug_ref, e_ref, tile_ref, acc_ref, m_ref, alpha_ref, s_ref, p_ref, xpad_ref):
    i = pl.program_id(1)
    t0 = i * TQ
    slopes2 = [s * LOG2E for s in SLOPES]

    @pl.when((pl.program_id(0) == 0) & (i == 0))
    def _():
        sl = lax.broadcasted_iota(jnp.int32, (TK, 2 * TQ), 0)
        tl = lax.broadcasted_iota(jnp.int32, (TK, 2 * TQ), 1) % TQ
        allowed = (sl // CHUNK) <= (tl // CHUNK)
        rel = (tl - jnp.abs(tl - sl) - sl).astype(F32)
        crow = lax.broadcasted_iota(jnp.int32, (2 * HEAD_DIM, 2 * TQ), 0)
        for h in range(N_HEADS):
            tile_ref[h] = jnp.where(allowed, slopes2[h] * rel, -jnp.inf)
            c1, c2, c3 = _bf16_split3(slopes2[h])
            consts = jnp.where(crow == 0, c1, jnp.where(crow == 1, c2, jnp.where(crow == 2, c3, 0.0)))
            qaug_ref[h, 2 * HEAD_DIM:, :] = consts.astype(BF16)
        lane = lax.broadcasted_iota(jnp.int32, (TK, 2 * HEAD_DIM), 1)
        row = lax.broadcasted_iota(jnp.int32, (TK, 2 * HEAD_DIM), 0)
        e_ref[...] = jnp.where(lane < 3, row, 0).astype(F32).astype(BF16)

    seq = glu_ref.shape[1]
    n_steps = seq // TQ
    n_trips = n_steps * (n_steps - 1) // 2
    n_extra = seq // CONV_ROWS - n_trips
    assert 0 <= n_extra <= n_steps

    @pl.when(i == 0)
    def _():
        xpad_ref[:, 0:CONV_HALO, :] = jnp.zeros((D_CONV // 128, CONV_HALO, 128), F32)

        def fill(r, carry):
            r0 = pl.multiple_of(r * TQ, TQ)
            x = glu_ref[0, pl.ds(r0, TQ), :].astype(F32)
            for c in range(D_CONV // 128):
                xpad_ref[c, pl.ds(CONV_HALO + r0, TQ), :] = x[:, c * 128:(c + 1) * 128]
            return carry

        lax.fori_loop(0, seq // TQ, fill, 0)

    def conv_piece(q):
        r0 = pl.multiple_of(q * CONV_ROWS, CONV_ROWS)
        shift = CONV_HALO - (CONV_WIDTH - 1)
        accs = []
        for c in range(D_CONV // 128):
            acc = jnp.zeros((CONV_ROWS, 128), F32)
            for j in range(CONV_WIDTH):
                xs = xpad_ref[c, pl.ds(r0 + (shift + j), CONV_ROWS), :]
                acc = acc + xs * cw_ref[j:j + 1, c * 128:(c + 1) * 128]
            accs.append(acc)
        y = jnp.concatenate(accs, axis=1) + cb_ref[...]
        mu = jnp.mean(y, axis=-1, keepdims=True)
        yc = y - mu
        var = jnp.mean(yc * yc, axis=-1, keepdims=True)
        yn = yc * lax.rsqrt(var + EPS) * lng_ref[...] + lnb_ref[...]
        hc_ref[0, pl.ds(r0, CONV_ROWS), :] = _silu(yn).astype(BF16)

    zeros = jnp.zeros((HEAD_DIM, TQ), BF16)
    for h in range(N_HEADS):
        qt = qt_ref[0, h]
        qaug_ref[h, 0:2 * HEAD_DIM, :] = jnp.concatenate(
            [jnp.concatenate([qt[:HEAD_DIM], zeros], axis=0),
             jnp.concatenate([zeros, qt[HEAD_DIM:]], axis=0)], axis=1)

    ones = jnp.ones((16, TK), BF16)

    def scores(h, j):
        j0 = pl.multiple_of(j * TK, TK)
        lhs = jnp.concatenate([k_ref[0, pl.ds(j0, TK), h * V_DIM:(h + 1) * V_DIM], e_ref[...]], axis=1)
        return jnp.dot(lhs, qaug_ref[h], preferred_element_type=F32)

    def softmax(h, j, first):
        s = s_ref[h]
        if first:
            s = s + tile_ref[h]
            m_new = jnp.max(s, axis=0, keepdims=True)
            base = m_new
            alpha_ref[h] = jnp.zeros((1, 2 * TQ), F32)
        else:
            shift = slopes2[h] * jnp.full((1, 2 * TQ), j * TK - t0, jnp.int32).astype(F32)
            m_old = m_ref[h]
            m_new = jnp.maximum(m_old, jnp.max(s, axis=0, keepdims=True) + shift)
            base = m_new - shift
            alpha_ref[h] = jnp.exp2(m_old - m_new)
        p_ref[h] = jnp.exp2(s - base).astype(BF16)
        m_ref[h] = m_new

    def accumulate(h, j):
        lhs_v = jnp.concatenate([vt_ref[j, h], ones], axis=0)
        pv = jnp.dot(lhs_v, p_ref[h], preferred_element_type=F32)
        acc_ref[h] = acc_ref[h] * alpha_ref[h] + pv

    for h in range(N_HEADS):
        acc_ref[h] = jnp.zeros((V_DIM + 16, 2 * TQ), F32)
        s_ref[h] = scores(h, i)
    for h in range(N_HEADS):
        softmax(h, i, True)
        s_ref[h] = scores(h, 0)

    def body(t, carry):
        prev = jnp.where(t == 0, i, t - 1)
        nxt = jnp.minimum(t + 1, i)
        for h in range(N_HEADS):
            s_nxt = scores(h, nxt)
            accumulate(h, prev)
            softmax(h, t, False)
            s_ref[h] = s_nxt
        conv_piece(i * (i - 1) // 2 + t)
        return carry

    lax.fori_loop(0, i, body, 0)
    for h in range(N_HEADS):
        accumulate(h, jnp.where(i == 0, i, i - 1))

    @pl.when(i < n_extra)
    def _():
        conv_piece(n_trips + i)

    lam = (jnp.exp(jnp.sum(lq1_ref[...] * lk1_ref[...], axis=-1, keepdims=True))
           - jnp.exp(jnp.sum(lq2_ref[...] * lk2_ref[...], axis=-1, keepdims=True))
           + LAMBDA_INIT)
    for h in range(N_HEADS):
        hs = slice(h * V_DIM, (h + 1) * V_DIM)
        a = acc_ref[h]
        r = 1.0 / a[V_DIM:V_DIM + 1, :]
        o = (a[:V_DIM, :TQ] * r[:, :TQ] - lam * (a[:V_DIM, TQ:] * r[:, TQ:])).T
        o = o * lax.rsqrt(jnp.mean(o * o, axis=-1, keepdims=True) + EPS)
        o = o * subg_ref[...] * (1.0 - LAMBDA_INIT)
        o_ref[0, :, hs] = (o * za_ref[0, :, hs].astype(F32)).astype(BF16)


def _attn(qt, k, vt, za, lq1, lk1, lq2, lk2, subg, glu, conv_w, conv_b, ln_g, ln_b):
    B, S, _ = k.shape
    nq = S // TQ
    blk = lambda b, i: (b, i, 0)
    whole = lambda b, i: (b, 0, 0)
    fixed = lambda b, i: (0, 0)
    return pl.pallas_call(
        _attn_kernel,
        grid=(B, nq),
        in_specs=[
            pl.BlockSpec((1, N_HEADS, 2 * HEAD_DIM, TQ), lambda b, i: (b * nq + i, 0, 0, 0)),
            pl.BlockSpec((1, S, D_QK), whole),
            pl.BlockSpec((S // TK, N_HEADS, V_DIM, TK), lambda b, i: (b, 0, 0, 0)),
            pl.BlockSpec((1, TQ, D_ATTN), blk),
            pl.BlockSpec((1, HEAD_DIM), fixed),
            pl.BlockSpec((1, HEAD_DIM), fixed),
            pl.BlockSpec((1, HEAD_DIM), fixed),
            pl.BlockSpec((1, HEAD_DIM), fixed),
            pl.BlockSpec((1, V_DIM), fixed),
            pl.BlockSpec((1, S, D_CONV), whole),
            pl.BlockSpec((CONV_WIDTH, D_CONV), fixed),
            pl.BlockSpec((1, D_CONV), fixed),
            pl.BlockSpec((1, D_CONV), fixed),
            pl.BlockSpec((1, D_CONV), fixed),
        ],
        out_specs=[pl.BlockSpec((1, TQ, D_ATTN), blk), pl.BlockSpec((1, S, D_CONV), whole)],
        out_shape=[jax.ShapeDtypeStruct((B, S, D_ATTN), BF16), jax.ShapeDtypeStruct((B, S, D_CONV), BF16)],
        scratch_shapes=[
            pltpu.VMEM((N_HEADS, 4 * HEAD_DIM, 2 * TQ), BF16),
            pltpu.VMEM((TK, 2 * HEAD_DIM), BF16),
            pltpu.VMEM((N_HEADS, TK, 2 * TQ), F32),
            pltpu.VMEM((N_HEADS, V_DIM + 16, 2 * TQ), F32),
            pltpu.VMEM((N_HEADS, 1, 2 * TQ), F32),
            pltpu.VMEM((N_HEADS, 1, 2 * TQ), F32),
            pltpu.VMEM((N_HEADS, TK, 2 * TQ), F32),
            pltpu.VMEM((N_HEADS, TK, 2 * TQ), BF16),
            pltpu.VMEM((D_CONV // 128, CONV_HALO + S, 128), F32),
        ],
        compiler_params=pltpu.CompilerParams(
            dimension_semantics=("arbitrary", "arbitrary"), vmem_limit_bytes=VMEM_LIMIT_BYTES),
        name="attn",
    )(qt, k, vt, za, lq1, lk1, lq2, lk2, subg, glu, conv_w, conv_b, ln_g, ln_b)


def _out_kernel(x_ref, hc_ref, zc_ref, ha_ref, g_ref, wc_ref, wa_ref, wo_ref, gpost_ref, o_ref):
    hc = (hc_ref[...].astype(F32) * zc_ref[...].astype(F32)).astype(BF16)
    yc = jnp.dot(hc, wc_ref[...], preferred_element_type=F32)
    ya = jnp.dot(ha_ref[...], wa_ref[...], preferred_element_type=F32)
    mixed = (g_ref[:, :D_MODEL].astype(F32) * yc + g_ref[:, D_MODEL:].astype(F32) * ya).astype(BF16)
    out = jnp.dot(mixed, wo_ref[...], preferred_element_type=F32)
    ms = jnp.mean(out * out, axis=-1, keepdims=True)
    o_ref[...] = x_ref[...] + out * lax.rsqrt(ms + EPS) * gpost_ref[...]


def _out(x2, hc, zc, ha, g, wc, wa, wo, gpost):
    T = x2.shape[0]
    row = lambda i: (i, 0)
    fixed = lambda i: (0, 0)
    return pl.pallas_call(
        _out_kernel,
        grid=(T // TM_OUT,),
        in_specs=[
            pl.BlockSpec((TM_OUT, D_MODEL), row),
            pl.BlockSpec((TM_OUT, D_CONV), row),
            pl.BlockSpec((TM_OUT, D_CONV), row),
            pl.BlockSpec((TM_OUT, D_ATTN), row),
            pl.BlockSpec((TM_OUT, 2 * D_MODEL), row),
            pl.BlockSpec((D_CONV, D_MODEL), fixed),
            pl.BlockSpec((D_ATTN, D_MODEL), fixed),
            pl.BlockSpec((D_MODEL, D_MODEL), fixed),
            pl.BlockSpec((1, D_MODEL), fixed),
        ],
        out_specs=pl.BlockSpec((TM_OUT, D_MODEL), row),
        out_shape=jax.ShapeDtypeStruct((T, D_MODEL), F32),
        compiler_params=pltpu.CompilerParams(
            dimension_semantics=("arbitrary",), vmem_limit_bytes=VMEM_LIMIT_BYTES),
        name="out",
    )(x2, hc, zc, ha, g, wc, wa, wo, gpost)


def kernel(x, w_in, conv_w, conv_b, conv_ln_g, conv_ln_b, w_conv_proj, lambda_q1, lambda_k1,
           lambda_q2, lambda_k2, subln_g, w_attn_proj, w_out, norm_pre_g, norm_post_g):
    B, S, D = x.shape
    assert (D, w_in.shape[0]) == (D_MODEL, 1) and TQ == TK
    assert S % TQ == 0 and (B * S) % TM_IN == 0
    T = B * S
    x2 = x.reshape(T, D)
    glu, zc, qt, k, vt, za, g = _inproj(x2, norm_pre_g, w_in[0].astype(BF16))
    seq = lambda a: a.reshape(B, S, a.shape[-1])
    ha, hc = _attn(qt, seq(k), vt, seq(za), lambda_q1, lambda_k1, lambda_q2, lambda_k2, subln_g,
                   seq(glu), conv_w[0], conv_b, conv_ln_g, conv_ln_b)
    y = _out(x2, hc.reshape(T, D_CONV), zc, ha.reshape(T, D_ATTN), g,
             w_conv_proj[0].astype(BF16), w_attn_proj[0].astype(BF16), w_out[0].astype(BF16),
             norm_post_g)
    return y.reshape(B, S, D)
```

```python
import math
import struct

import jax
import jax.numpy as jnp
from jax import lax
from jax.experimental import pallas as pl
from jax.experimental.pallas import tpu as pltpu

D_MODEL = 1024
CHUNK = 64
CONV_WIDTH = 31
D_CONV = 512
N_HEADS = 4
HEAD_DIM = 64
V_DIM = 128
D_ATTN = 512
D_QK = 512
EPS = 1e-6
LAMBDA_INIT = 0.8 - 0.6 * math.exp(-0.3 * 0)
LOG2E = math.log2(math.e)
SLOPES = tuple(2.0 ** (-8.0 * (h + 1) / N_HEADS) for h in range(N_HEADS))

BF16 = jnp.bfloat16
F32 = jnp.float32

VMEM_LIMIT_BYTES = 56 * 1024 * 1024

TM_IN = 1024
TS_CONV = 256
CONV_HALO = 32
CONV_ROWS = 64
TQ = 256
TK = 256
TM_OUT = 1024


def _sigmoid(x):
    return 1.0 / (1.0 + jnp.exp(-x))


def _silu(x):
    return x * _sigmoid(x)


def _inproj_kernel(x_ref, gpre_ref, w_ref, glu_ref, zc_ref, qt_ref, k_ref, vt_ref, za_ref, g_ref):
    x = x_ref[...]
    ms = jnp.mean(x * x, axis=-1, keepdims=True)
    h = (x * lax.rsqrt(ms + EPS) * gpre_ref[...]).astype(BF16)

    def proj(c0, width):
        return jnp.dot(h, w_ref[:, c0:c0 + width], preferred_element_type=F32)

    def store_transposed(t_ref, y):
        for blk in range(TM_IN // TK):
            for hd in range(N_HEADS):
                t_ref[blk, hd] = y[blk * TK:(blk + 1) * TK, hd * V_DIM:(hd + 1) * V_DIM].T.astype(BF16)

    a = proj(0, D_CONV)
    b = proj(D_CONV, D_CONV)
    glu_ref[...] = (a * _sigmoid(b)).astype(BF16)
    zc_ref[...] = _silu(proj(2 * D_CONV, D_CONV)).astype(BF16)
    c0 = 3 * D_CONV
    store_transposed(qt_ref, proj(c0, D_QK) * (LOG2E / math.sqrt(HEAD_DIM)))
    k_ref[...] = proj(c0 + D_QK, D_QK).astype(BF16)
    store_transposed(vt_ref, proj(c0 + 2 * D_QK, D_ATTN))
    za_ref[...] = _silu(proj(c0 + 2 * D_QK + D_ATTN, D_ATTN)).astype(BF16)
    c0 = c0 + 2 * D_QK + 2 * D_ATTN
    for j in range(2 * D_MODEL // 512):
        g_ref[:, j * 512:(j + 1) * 512] = _sigmoid(proj(c0 + j * 512, 512)).astype(BF16)


def _inproj(x2, gpre, w_in):
    T = x2.shape[0]
    d_in = w_in.shape[1]
    row = lambda i: (i, 0)
    fixed = lambda i: (0, 0)
    tr_shape = (T // TK, N_HEADS, V_DIM, TK)
    tr_spec = pl.BlockSpec((TM_IN // TK, N_HEADS, V_DIM, TK), lambda i: (i, 0, 0, 0))
    rows = lambda w: (pl.BlockSpec((TM_IN, w), row), jax.ShapeDtypeStruct((T, w), BF16))
    tr = (tr_spec, jax.ShapeDtypeStruct(tr_shape, BF16))
    outs = (rows(D_CONV), rows(D_CONV), tr, rows(D_QK), tr, rows(D_ATTN), rows(2 * D_MODEL))
    return pl.pallas_call(
        _inproj_kernel,
        grid=(T // TM_IN,),
        in_specs=[
            pl.BlockSpec((TM_IN, D_MODEL), row),
            pl.BlockSpec((1, D_MODEL), fixed),
            pl.BlockSpec((D_MODEL, d_in), fixed, pipeline_mode=pl.Buffered(1)),
        ],
        out_specs=[o[0] for o in outs],
        out_shape=[o[1] for o in outs],
        compiler_params=pltpu.CompilerParams(
            dimension_semantics=("arbitrary",), vmem_limit_bytes=VMEM_LIMIT_BYTES),
        name="inproj",
    )(x2, gpre, w_in)


def _conv_kernel(glu_ref, halo_ref, zc_ref, w_ref, b_ref, lng_ref, lnb_ref, o_ref, xpad_ref, acc_ref):
    i = pl.program_id(1)
    halo = jnp.where(i == 0, 0.0, halo_ref[0].astype(F32))
    cur = glu_ref[0].astype(F32)
    for c in range(D_CONV // 128):
        xpad_ref[c, 0:CONV_HALO, :] = halo[:, c * 128:(c + 1) * 128]
        xpad_ref[c, CONV_HALO:, :] = cur[:, c * 128:(c + 1) * 128]

    shift = CONV_HALO - (CONV_WIDTH - 1)

    def rows(r, carry):
        r0 = pl.multiple_of(r * CONV_ROWS, CONV_ROWS)
        for c in range(D_CONV // 128):
            acc = jnp.zeros((CONV_ROWS, 128), F32)
            for j in range(CONV_WIDTH):
                xs = xpad_ref[c, pl.ds(r0 + (shift + j), CONV_ROWS), :]
                acc = acc + xs * w_ref[j:j + 1, c * 128:(c + 1) * 128]
            acc_ref[pl.ds(r0, CONV_ROWS), c * 128:(c + 1) * 128] = acc
        return carry

    lax.fori_loop(0, TS_CONV // CONV_ROWS, rows, 0)

    y = acc_ref[...] + b_ref[...]
    mu = jnp.mean(y, axis=-1, keepdims=True)
    yc = y - mu
    var = jnp.mean(yc * yc, axis=-1, keepdims=True)
    yn = yc * lax.rsqrt(var + EPS) * lng_ref[...] + lnb_ref[...]
    o_ref[0] = (_silu(yn) * zc_ref[0].astype(F32)).astype(BF16)


def _conv(glu, zc, conv_w, conv_b, ln_g, ln_b):
    B, S, _ = glu.shape
    blocks_per_halo = TS_CONV // CONV_HALO
    cur = lambda b, i: (b, i, 0)
    prev = lambda b, i: (b, jnp.maximum(i * blocks_per_halo - 1, 0), 0)
    fixed = lambda b, i: (0, 0)
    return pl.pallas_call(
        _conv_kernel,
        grid=(B, S // TS_CONV),
        in_specs=[
            pl.BlockSpec((1, TS_CONV, D_CONV), cur),
            pl.BlockSpec((1, CONV_HALO, D_CONV), prev),
            pl.BlockSpec((1, TS_CONV, D_CONV), cur),
            pl.BlockSpec((CONV_WIDTH, D_CONV), fixed),
            pl.BlockSpec((1, D_CONV), fixed),
            pl.BlockSpec((1, D_CONV), fixed),
            pl.BlockSpec((1, D_CONV), fixed),
        ],
        out_specs=pl.BlockSpec((1, TS_CONV, D_CONV), cur),
        out_shape=jax.ShapeDtypeStruct((B, S, D_CONV), BF16),
        scratch_shapes=[
            pltpu.VMEM((D_CONV // 128, TS_CONV + CONV_HALO, 128), F32),
            pltpu.VMEM((TS_CONV, D_CONV), F32),
        ],
        compiler_params=pltpu.CompilerParams(
            dimension_semantics=("arbitrary", "arbitrary"), vmem_limit_bytes=VMEM_LIMIT_BYTES),
        name="conv",
    )(glu, glu, zc, conv_w, conv_b, ln_g, ln_b)


def _bf16_round(x):
    bits = struct.unpack("<I", struct.pack("<f", x))[0]
    bits = (bits + 0x7FFF + ((bits >> 16) & 1)) & 0xFFFF0000
    return struct.unpack("<f", struct.pack("<I", bits))[0]


def _bf16_split3(c):
    c1 = _bf16_round(c)
    c2 = _bf16_round(c - c1)
    c3 = _bf16_round(c - c1 - c2)
    return c1, c2, c3


def _attn_kernel(qt_ref, k_ref, vt_ref, za_ref, lq1_ref, lk1_ref, lq2_ref, lk2_ref, subg_ref,
                 o_ref, qaug_ref, e_ref, tile_ref, acc_ref, m_ref, alpha_ref, s_ref, p_ref):
    i = pl.program_id(1)
    t0 = i * TQ
    slopes2 = [s * LOG2E for s in SLOPES]

    @pl.when((pl.program_id(0) == 0) & (i == 0))
    def _():
        sl = lax.broadcasted_iota(jnp.int32, (TK, 2 * TQ), 0)
        tl = lax.broadcasted_iota(jnp.int32, (TK, 2 * TQ), 1) % TQ
        allowed = (sl // CHUNK) <= (tl // CHUNK)
        rel = (tl - jnp.abs(tl - sl) - sl).astype(F32)
        crow = lax.broadcasted_iota(jnp.int32, (2 * HEAD_DIM, 2 * TQ), 0)
        for h in range(N_HEADS):
            tile_ref[h] = jnp.where(allowed, slopes2[h] * rel, -jnp.inf)
            c1, c2, c3 = _bf16_split3(slopes2[h])
            consts = jnp.where(crow == 0, c1, jnp.where(crow == 1, c2, jnp.where(crow == 2, c3, 0.0)))
            qaug_ref[h, 2 * HEAD_DIM:, :] = consts.astype(BF16)
        lane = lax.broadcasted_iota(jnp.int32, (TK, 2 * HEAD_DIM), 1)
        row = lax.broadcasted_iota(jnp.int32, (TK, 2 * HEAD_DIM), 0)
        e_ref[...] = jnp.where(lane < 3, row, 0).astype(F32).astype(BF16)

    zeros = jnp.zeros((HEAD_DIM, TQ), BF16)
    for h in range(N_HEADS):
        qt = qt_ref[0, h]
        qaug_ref[h, 0:2 * HEAD_DIM, :] = jnp.concatenate(
            [jnp.concatenate([qt[:HEAD_DIM], zeros], axis=0),
             jnp.concatenate([zeros, qt[HEAD_DIM:]], axis=0)], axis=1)

    ones = jnp.ones((16, TK), BF16)

    def scores(h, j):
        j0 = pl.multiple_of(j * TK, TK)
        lhs = jnp.concatenate([k_ref[0, pl.ds(j0, TK), h * V_DIM:(h + 1) * V_DIM], e_ref[...]], axis=1)
        return jnp.dot(lhs, qaug_ref[h], preferred_element_type=F32)

    def softmax(h, j, first):
        s = s_ref[h]
        if first:
            s = s + tile_ref[h]
            m_new = jnp.max(s, axis=0, keepdims=True)
            base = m_new
            alpha_ref[h] = jnp.zeros((1, 2 * TQ), F32)
        else:
            shift = slopes2[h] * jnp.full((1, 2 * TQ), j * TK - t0, jnp.int32).astype(F32)
            m_old = m_ref[h]
            m_new = jnp.maximum(m_old, jnp.max(s, axis=0, keepdims=True) + shift)
            base = m_new - shift
            alpha_ref[h] = jnp.exp2(m_old - m_new)
        p_ref[h] = jnp.exp2(s - base).astype(BF16)
        m_ref[h] = m_new

    def accumulate(h, j):
        lhs_v = jnp.concatenate([vt_ref[j, h], ones], axis=0)
        pv = jnp.dot(lhs_v, p_ref[h], preferred_element_type=F32)
        acc_ref[h] = acc_ref[h] * alpha_ref[h] + pv

    for h in range(N_HEADS):
        acc_ref[h] = jnp.zeros((V_DIM + 16, 2 * TQ), F32)
        s_ref[h] = scores(h, i)
        s_nxt = scores(h, 0)
        softmax(h, i, True)
        s_ref[h] = s_nxt

    def body(t, carry):
        prev = jnp.where(t == 0, i, t - 1)
        nxt = jnp.minimum(t + 1, i)
        for h in range(N_HEADS):
            s_nxt = scores(h, nxt)
            accumulate(h, prev)
            softmax(h, t, False)
            s_ref[h] = s_nxt
        return carry

    lax.fori_loop(0, i, body, 0)
    lam = (jnp.exp(jnp.sum(lq1_ref[...] * lk1_ref[...], axis=-1, keepdims=True))
           - jnp.exp(jnp.sum(lq2_ref[...] * lk2_ref[...], axis=-1, keepdims=True))
           + LAMBDA_INIT)
    for h in range(N_HEADS):
        accumulate(h, jnp.where(i == 0, i, i - 1))
        hs = slice(h * V_DIM, (h + 1) * V_DIM)
        a = acc_ref[h]
        r = 1.0 / a[V_DIM:V_DIM + 1, :]
        ot = a[:V_DIM, :TQ] * r[:, :TQ] - lam * (a[:V_DIM, TQ:] * r[:, TQ:])
        ot = ot * lax.rsqrt(jnp.mean(ot * ot, axis=0, keepdims=True) + EPS)
        o = ot.T * (subg_ref[...] * (1.0 - LAMBDA_INIT))
        o_ref[0, :, hs] = (o * za_ref[0, :, hs].astype(F32)).astype(BF16)


def _attn(qt, k, vt, za, lq1, lk1, lq2, lk2, subg):
    B, S, _ = k.shape
    nq = S // TQ
    blk = lambda b, i: (b, i, 0)
    whole = lambda b, i: (b, 0, 0)
    fixed = lambda b, i: (0, 0)
    return pl.pallas_call(
        _attn_kernel,
        grid=(B, nq),
        in_specs=[
            pl.BlockSpec((1, N_HEADS, 2 * HEAD_DIM, TQ), lambda b, i: (b * nq + i, 0, 0, 0)),
            pl.BlockSpec((1, S, D_QK), whole),
            pl.BlockSpec((S // TK, N_HEADS, V_DIM, TK), lambda b, i: (b, 0, 0, 0)),
            pl.BlockSpec((1, TQ, D_ATTN), blk),
            pl.BlockSpec((1, HEAD_DIM), fixed),
            pl.BlockSpec((1, HEAD_DIM), fixed),
            pl.BlockSpec((1, HEAD_DIM), fixed),
            pl.BlockSpec((1, HEAD_DIM), fixed),
            pl.BlockSpec((1, V_DIM), fixed),
        ],
        out_specs=pl.BlockSpec((1, TQ, D_ATTN), blk),
        out_shape=jax.ShapeDtypeStruct((B, S, D_ATTN), BF16),
        scratch_shapes=[
            pltpu.VMEM((N_HEADS, 4 * HEAD_DIM, 2 * TQ), BF16),
            pltpu.VMEM((TK, 2 * HEAD_DIM), BF16),
            pltpu.VMEM((N_HEADS, TK, 2 * TQ), F32),
            pltpu.VMEM((N_HEADS, V_DIM + 16, 2 * TQ), F32),
            pltpu.VMEM((N_HEADS, 1, 2 * TQ), F32),
            pltpu.VMEM((N_HEADS, 1, 2 * TQ), F32),
            pltpu.VMEM((N_HEADS, TK, 2 * TQ), F32),
            pltpu.VMEM((N_HEADS, TK, 2 * TQ), BF16),
        ],
        compiler_params=pltpu.CompilerParams(
            dimension_semantics=("arbitrary", "arbitrary"), vmem_limit_bytes=VMEM_LIMIT_BYTES),
        name="attn",
    )(qt, k, vt, za, lq1, lk1, lq2, lk2, subg)


def _out_kernel(x_ref, hc_ref, ha_ref, g_ref, wc_ref, wa_ref, wo_ref, gpost_ref, o_ref):
    yc = jnp.dot(hc_ref[...], wc_ref[...], preferred_element_type=F32)
    ya = jnp.dot(ha_ref[...], wa_ref[...], preferred_element_type=F32)
    mixed = (g_ref[:, :D_MODEL].astype(F32) * yc + g_ref[:, D_MODEL:].astype(F32) * ya).astype(BF16)
    out = jnp.dot(mixed, wo_ref[...], preferred_element_type=F32)
    ms = jnp.mean(out * out, axis=-1, keepdims=True)
    o_ref[...] = x_ref[...] + out * lax.rsqrt(ms + EPS) * gpost_ref[...]


def _out(x2, hc, ha, g, wc, wa, wo, gpost):
    T = x2.shape[0]
    row = lambda i: (i, 0)
    fixed = lambda i: (0, 0)
    return pl.pallas_call(
        _out_kernel,
        grid=(T // TM_OUT,),
        in_specs=[
            pl.BlockSpec((TM_OUT, D_MODEL), row),
            pl.BlockSpec((TM_OUT, D_CONV), row),
            pl.BlockSpec((TM_OUT, D_ATTN), row),
            pl.BlockSpec((TM_OUT, 2 * D_MODEL), row),
            pl.BlockSpec((D_CONV, D_MODEL), fixed),
            pl.BlockSpec((D_ATTN, D_MODEL), fixed),
            pl.BlockSpec((D_MODEL, D_MODEL), fixed),
            pl.BlockSpec((1, D_MODEL), fixed),
        ],
        out_specs=pl.BlockSpec((TM_OUT, D_MODEL), row),
        out_shape=jax.ShapeDtypeStruct((T, D_MODEL), F32),
        compiler_params=pltpu.CompilerParams(
            dimension_semantics=("arbitrary",), vmem_limit_bytes=VMEM_LIMIT_BYTES),
        name="out",
    )(x2, hc, ha, g, wc, wa, wo, gpost)


def kernel(x, w_in, conv_w, conv_b, conv_ln_g, conv_ln_b, w_conv_proj, lambda_q1, lambda_k1,
           lambda_q2, lambda_k2, subln_g, w_attn_proj, w_out, norm_pre_g, norm_post_g):
    B, S, D = x.shape
    assert (D, w_in.shape[0]) == (D_MODEL, 1) and TQ == TK
    assert S % max(TQ, TS_CONV) == 0 and (B * S) % TM_IN == 0
    T = B * S
    x2 = x.reshape(T, D)
    glu, zc, qt, k, vt, za, g = _inproj(x2, norm_pre_g, w_in[0].astype(BF16))
    seq = lambda a: a.reshape(B, S, a.shape[-1])
    hc = _conv(seq(glu), seq(zc), conv_w[0], conv_b, conv_ln_g, conv_ln_b)
    ha = _attn(qt, seq(k), vt, seq(za), lambda_q1, lambda_k1, lambda_q2, lambda_k2, subln_g)
    y = _out(x2, hc.reshape(T, D_CONV), ha.reshape(T, D_ATTN), g,
             w_conv_proj[0].astype(BF16), w_attn_proj[0].astype(BF16), w_out[0].astype(BF16),
             norm_post_g)
    return y.reshape(B, S, D)
```

```python
import math

import jax
import jax.numpy as jnp
from jax import lax
from jax.experimental import pallas as pl
from jax.experimental.pallas import tpu as pltpu

D_MODEL = 1024
CHUNK = 64
CONV_WIDTH = 31
D_CONV = 512
N_HEADS = 4
HEAD_DIM = 64
V_DIM = 128
D_ATTN = 512
D_QK = 512
EPS = 1e-6
LAMBDA_INIT = 0.8 - 0.6 * math.exp(-0.3 * 0)
LOG2E = math.log2(math.e)
SLOPES = tuple(2.0 ** (-8.0 * (h + 1) / N_HEADS) for h in range(N_HEADS))

BF16 = jnp.bfloat16
F32 = jnp.float32

VMEM_LIMIT_BYTES = 56 * 1024 * 1024

TM_IN = 1024
TS_CONV = 256
CONV_HALO = 32
CONV_ROWS = 64
TQ = 256
TK = 256
TM_OUT = 1024


def _sigmoid(x):
    return 1.0 / (1.0 + jnp.exp(-x))


def _silu(x):
    return x * _sigmoid(x)


def _inproj_kernel(x_ref, gpre_ref, w_ref, glu_ref, zc_ref, qt_ref, k_ref, vt_ref, za_ref, g_ref):
    x = x_ref[...]
    ms = jnp.mean(x * x, axis=-1, keepdims=True)
    h = (x * lax.rsqrt(ms + EPS) * gpre_ref[...]).astype(BF16)

    def proj(c0, width):
        return jnp.dot(h, w_ref[:, c0:c0 + width], preferred_element_type=F32)

    def store_transposed(t_ref, y):
        for blk in range(TM_IN // TK):
            for hd in range(N_HEADS):
                t_ref[blk, hd] = y[blk * TK:(blk + 1) * TK, hd * V_DIM:(hd + 1) * V_DIM].T.astype(BF16)

    a = proj(0, D_CONV)
    b = proj(D_CONV, D_CONV)
    glu_ref[...] = (a * _sigmoid(b)).astype(BF16)
    zc_ref[...] = _silu(proj(2 * D_CONV, D_CONV)).astype(BF16)
    c0 = 3 * D_CONV
    store_transposed(qt_ref, proj(c0, D_QK) * (LOG2E / math.sqrt(HEAD_DIM)))
    k_ref[...] = proj(c0 + D_QK, D_QK).astype(BF16)
    store_transposed(vt_ref, proj(c0 + 2 * D_QK, D_ATTN))
    za_ref[...] = _silu(proj(c0 + 2 * D_QK + D_ATTN, D_ATTN)).astype(BF16)
    c0 = c0 + 2 * D_QK + 2 * D_ATTN
    for j in range(2 * D_MODEL // 512):
        g_ref[:, j * 512:(j + 1) * 512] = _sigmoid(proj(c0 + j * 512, 512)).astype(BF16)


def _inproj(x2, gpre, w_in):
    T = x2.shape[0]
    d_in = w_in.shape[1]
    row = lambda i: (i, 0)
    fixed = lambda i: (0, 0)
    tr_shape = (T // TK, N_HEADS, V_DIM, TK)
    tr_spec = pl.BlockSpec((TM_IN // TK, N_HEADS, V_DIM, TK), lambda i: (i, 0, 0, 0))
    rows = lambda w: (pl.BlockSpec((TM_IN, w), row), jax.ShapeDtypeStruct((T, w), BF16))
    tr = (tr_spec, jax.ShapeDtypeStruct(tr_shape, BF16))
    outs = (rows(D_CONV), rows(D_CONV), tr, rows(D_QK), tr, rows(D_ATTN), rows(2 * D_MODEL))
    return pl.pallas_call(
        _inproj_kernel,
        grid=(T // TM_IN,),
        in_specs=[
            pl.BlockSpec((TM_IN, D_MODEL), row),
            pl.BlockSpec((1, D_MODEL), fixed),
            pl.BlockSpec((D_MODEL, d_in), fixed, pipeline_mode=pl.Buffered(1)),
        ],
        out_specs=[o[0] for o in outs],
        out_shape=[o[1] for o in outs],
        compiler_params=pltpu.CompilerParams(
            dimension_semantics=("arbitrary",), vmem_limit_bytes=VMEM_LIMIT_BYTES),
        name="inproj",
    )(x2, gpre, w_in)


def _conv_kernel(glu_ref, halo_ref, zc_ref, w_ref, b_ref, lng_ref, lnb_ref, o_ref, xpad_ref, acc_ref):
    i = pl.program_id(1)
    halo = jnp.where(i == 0, 0.0, halo_ref[0].astype(F32))
    cur = glu_ref[0].astype(F32)
    for c in range(D_CONV // 128):
        xpad_ref[c, 0:CONV_HALO, :] = halo[:, c * 128:(c + 1) * 128]
        xpad_ref[c, CONV_HALO:, :] = cur[:, c * 128:(c + 1) * 128]

    shift = CONV_HALO - (CONV_WIDTH - 1)

    def rows(r, carry):
        r0 = pl.multiple_of(r * CONV_ROWS, CONV_ROWS)
        for c in range(D_CONV // 128):
            acc = jnp.zeros((CONV_ROWS, 128), F32)
            for j in range(CONV_WIDTH):
                xs = xpad_ref[c, pl.ds(r0 + (shift + j), CONV_ROWS), :]
                acc = acc + xs * w_ref[j:j + 1, c * 128:(c + 1) * 128]
            acc_ref[pl.ds(r0, CONV_ROWS), c * 128:(c + 1) * 128] = acc
        return carry

    lax.fori_loop(0, TS_CONV // CONV_ROWS, rows, 0)

    y = acc_ref[...] + b_ref[...]
    mu = jnp.mean(y, axis=-1, keepdims=True)
    yc = y - mu
    var = jnp.mean(yc * yc, axis=-1, keepdims=True)
    yn = yc * lax.rsqrt(var + EPS) * lng_ref[...] + lnb_ref[...]
    o_ref[0] = (_silu(yn) * zc_ref[0].astype(F32)).astype(BF16)


def _conv(glu, zc, conv_w, conv_b, ln_g, ln_b):
    B, S, _ = glu.shape
    blocks_per_halo = TS_CONV // CONV_HALO
    cur = lambda b, i: (b, i, 0)
    prev = lambda b, i: (b, jnp.maximum(i * blocks_per_halo - 1, 0), 0)
    fixed = lambda b, i: (0, 0)
    return pl.pallas_call(
        _conv_kernel,
        grid=(B, S // TS_CONV),
        in_specs=[
            pl.BlockSpec((1, TS_CONV, D_CONV), cur),
            pl.BlockSpec((1, CONV_HALO, D_CONV), prev),
            pl.BlockSpec((1, TS_CONV, D_CONV), cur),
            pl.BlockSpec((CONV_WIDTH, D_CONV), fixed),
            pl.BlockSpec((1, D_CONV), fixed),
            pl.BlockSpec((1, D_CONV), fixed),
            pl.BlockSpec((1, D_CONV), fixed),
        ],
        out_specs=pl.BlockSpec((1, TS_CONV, D_CONV), cur),
        out_shape=jax.ShapeDtypeStruct((B, S, D_CONV), BF16),
        scratch_shapes=[
            pltpu.VMEM((D_CONV // 128, TS_CONV + CONV_HALO, 128), F32),
            pltpu.VMEM((TS_CONV, D_CONV), F32),
        ],
        compiler_params=pltpu.CompilerParams(
            dimension_semantics=("arbitrary", "arbitrary"), vmem_limit_bytes=VMEM_LIMIT_BYTES),
        name="conv",
    )(glu, glu, zc, conv_w, conv_b, ln_g, ln_b)


def _attn_kernel(qt_ref, k_ref, vt_ref, za_ref, lq1_ref, lk1_ref, lq2_ref, lk2_ref, subg_ref,
                 o_ref, qaug_ref, colbias_ref, tile_ref, acc_ref, m_ref, alpha_ref, s_ref, p_ref):
    i = pl.program_id(1)
    t0 = i * TQ
    slopes2 = [s * LOG2E for s in SLOPES]

    @pl.when((pl.program_id(0) == 0) & (i == 0))
    def _():
        sl = lax.broadcasted_iota(jnp.int32, (TK, 2 * TQ), 0)
        tl = lax.broadcasted_iota(jnp.int32, (TK, 2 * TQ), 1) % TQ
        allowed = (sl // CHUNK) <= (tl // CHUNK)
        rel = (tl - jnp.abs(tl - sl)).astype(F32)
        row = lax.broadcasted_iota(jnp.int32, (TK, 128), 0).astype(F32)
        for h in range(N_HEADS):
            tile_ref[h] = jnp.where(allowed, slopes2[h] * rel, -jnp.inf)
            colbias_ref[h] = slopes2[h] * row

    zeros = jnp.zeros((HEAD_DIM, TQ), BF16)
    for h in range(N_HEADS):
        qt = qt_ref[0, h]
        qaug_ref[h, 0:2 * HEAD_DIM, :] = jnp.concatenate(
            [jnp.concatenate([qt[:HEAD_DIM], zeros], axis=0),
             jnp.concatenate([zeros, qt[HEAD_DIM:]], axis=0)], axis=1)

    ones = jnp.ones((16, TK), BF16)

    def scores(h, j):
        j0 = pl.multiple_of(j * TK, TK)
        lhs = k_ref[0, pl.ds(j0, TK), h * V_DIM:(h + 1) * V_DIM]
        return jnp.dot(lhs, qaug_ref[h], preferred_element_type=F32)

    def softmax(h, j, first):
        s = s_ref[h]
        if first:
            s = s + tile_ref[h]
            m_new = jnp.max(s, axis=0, keepdims=True)
            base = m_new
            alpha_ref[h] = jnp.zeros((1, 2 * TQ), F32)
        else:
            s = s + jnp.concatenate([colbias_ref[h]] * (2 * TQ // 128), axis=1)
            shift = slopes2[h] * jnp.full((1, 2 * TQ), j * TK - t0, jnp.int32).astype(F32)
            m_old = m_ref[h]
            m_new = jnp.maximum(m_old, jnp.max(s, axis=0, keepdims=True) + shift)
            base = m_new - shift
            alpha_ref[h] = jnp.exp2(m_old - m_new)
        p_ref[h] = jnp.exp2(s - base).astype(BF16)
        m_ref[h] = m_new

    def accumulate(h, j):
        lhs_v = jnp.concatenate([vt_ref[j, h], ones], axis=0)
        pv = jnp.dot(lhs_v, p_ref[h], preferred_element_type=F32)
        acc_ref[h] = acc_ref[h] * alpha_ref[h] + pv

    for h in range(N_HEADS):
        acc_ref[h] = jnp.zeros((V_DIM + 16, 2 * TQ), F32)
        s_ref[h] = scores(h, i)
        s_nxt = scores(h, 0)
        softmax(h, i, True)
        s_ref[h] = s_nxt

    def body(t, carry):
        prev = jnp.where(t == 0, i, t - 1)
        nxt = jnp.minimum(t + 1, i)
        for h in range(N_HEADS):
            s_nxt = scores(h, nxt)
            accumulate(h, prev)
            softmax(h, t, False)
            s_ref[h] = s_nxt
        return carry

    lax.fori_loop(0, i, body, 0)
    lam = (jnp.exp(jnp.sum(lq1_ref[...] * lk1_ref[...], axis=-1, keepdims=True))
           - jnp.exp(jnp.sum(lq2_ref[...] * lk2_ref[...], axis=-1, keepdims=True))
           + LAMBDA_INIT)
    for h in range(N_HEADS):
        accumulate(h, jnp.where(i == 0, i, i - 1))
        hs = slice(h * V_DIM, (h + 1) * V_DIM)
        a = acc_ref[h]
        r = 1.0 / a[V_DIM:V_DIM + 1, :]
        ot = a[:V_DIM, :TQ] * r[:, :TQ] - lam * (a[:V_DIM, TQ:] * r[:, TQ:])
        ot = ot * lax.rsqrt(jnp.mean(ot * ot, axis=0, keepdims=True) + EPS)
        o = ot.T * (subg_ref[...] * (1.0 - LAMBDA_INIT))
        o_ref[0, :, hs] = (o * za_ref[0, :, hs].astype(F32)).astype(BF16)


def _attn(qt, k, vt, za, lq1, lk1, lq2, lk2, subg):
    B, S, _ = k.shape
    nq = S // TQ
    blk = lambda b, i: (b, i, 0)
    whole = lambda b, i: (b, 0, 0)
    fixed = lambda b, i: (0, 0)
    return pl.pallas_call(
        _attn_kernel,
        grid=(B, nq),
        in_specs=[
            pl.BlockSpec((1, N_HEADS, 2 * HEAD_DIM, TQ), lambda b, i: (b * nq + i, 0, 0, 0)),
            pl.BlockSpec((1, S, D_QK), whole),
            pl.BlockSpec((S // TK, N_HEADS, V_DIM, TK), lambda b, i: (b, 0, 0, 0)),
            pl.BlockSpec((1, TQ, D_ATTN), blk),
            pl.BlockSpec((1, HEAD_DIM), fixed),
            pl.BlockSpec((1, HEAD_DIM), fixed),
            pl.BlockSpec((1, HEAD_DIM), fixed),
            pl.BlockSpec((1, HEAD_DIM), fixed),
            pl.BlockSpec((1, V_DIM), fixed),
        ],
        out_specs=pl.BlockSpec((1, TQ, D_ATTN), blk),
        out_shape=jax.ShapeDtypeStruct((B, S, D_ATTN), BF16),
        scratch_shapes=[
            pltpu.VMEM((N_HEADS, 2 * HEAD_DIM, 2 * TQ), BF16),
            pltpu.VMEM((N_HEADS, TK, 128), F32),
            pltpu.VMEM((N_HEADS, TK, 2 * TQ), F32),
            pltpu.VMEM((N_HEADS, V_DIM + 16, 2 * TQ), F32),
            pltpu.VMEM((N_HEADS, 1, 2 * TQ), F32),
            pltpu.VMEM((N_HEADS, 1, 2 * TQ), F32),
            pltpu.VMEM((N_HEADS, TK, 2 * TQ), F32),
            pltpu.VMEM((N_HEADS, TK, 2 * TQ), BF16),
        ],
        compiler_params=pltpu.CompilerParams(
            dimension_semantics=("arbitrary", "arbitrary"), vmem_limit_bytes=VMEM_LIMIT_BYTES),
        name="attn",
    )(qt, k, vt, za, lq1, lk1, lq2, lk2, subg)


def _out_kernel(x_ref, hc_ref, ha_ref, g_ref, wc_ref, wa_ref, wo_ref, gpost_ref, o_ref):
    yc = jnp.dot(hc_ref[...], wc_ref[...], preferred_element_type=F32)
    ya = jnp.dot(ha_ref[...], wa_ref[...], preferred_element_type=F32)
    mixed = (g_ref[:, :D_MODEL].astype(F32) * yc + g_ref[:, D_MODEL:].astype(F32) * ya).astype(BF16)
    out = jnp.dot(mixed, wo_ref[...], preferred_element_type=F32)
    ms = jnp.mean(out * out, axis=-1, keepdims=True)
    o_ref[...] = x_ref[...] + out * lax.rsqrt(ms + EPS) * gpost_ref[...]


def _out(x2, hc, ha, g, wc, wa, wo, gpost):
    T = x2.shape[0]
    row = lambda i: (i, 0)
    fixed = lambda i: (0, 0)
    return pl.pallas_call(
        _out_kernel,
        grid=(T // TM_OUT,),
        in_specs=[
            pl.BlockSpec((TM_OUT, D_MODEL), row),
            pl.BlockSpec((TM_OUT, D_CONV), row),
            pl.BlockSpec((TM_OUT, D_ATTN), row),
            pl.BlockSpec((TM_OUT, 2 * D_MODEL), row),
            pl.BlockSpec((D_CONV, D_MODEL), fixed),
            pl.BlockSpec((D_ATTN, D_MODEL), fixed),
            pl.BlockSpec((D_MODEL, D_MODEL), fixed),
            pl.BlockSpec((1, D_MODEL), fixed),
        ],
        out_specs=pl.BlockSpec((TM_OUT, D_MODEL), row),
        out_shape=jax.ShapeDtypeStruct((T, D_MODEL), F32),
        compiler_params=pltpu.CompilerParams(
            dimension_semantics=("arbitrary",), vmem_limit_bytes=VMEM_LIMIT_BYTES),
        name="out",
    )(x2, hc, ha, g, wc, wa, wo, gpost)


def kernel(x, w_in, conv_w, conv_b, conv_ln_g, conv_ln_b, w_conv_proj, lambda_q1, lambda_k1,
           lambda_q2, lambda_k2, subln_g, w_attn_proj, w_out, norm_pre_g, norm_post_g):
    B, S, D = x.shape
    assert (D, w_in.shape[0]) == (D_MODEL, 1) and TQ == TK
    assert S % max(TQ, TS_CONV) == 0 and (B * S) % TM_IN == 0
    T = B * S
    x2 = x.reshape(T, D)
    glu, zc, qt, k, vt, za, g = _inproj(x2, norm_pre_g, w_in[0].astype(BF16))
    seq = lambda a: a.reshape(B, S, a.shape[-1])
    hc = _conv(seq(glu), seq(zc), conv_w[0], conv_b, conv_ln_g, conv_ln_b)
    ha = _attn(qt, seq(k), vt, seq(za), lambda_q1, lambda_k1, lambda_q2, lambda_k2, subln_g)
    y = _out(x2, hc.reshape(T, D_CONV), ha.reshape(T, D_ATTN), g,
             w_conv_proj[0].astype(BF16), w_attn_proj[0].astype(BF16), w_out[0].astype(BF16),
             norm_post_g)
    return y.reshape(B, S, D)
```

```python
import math
import struct

import jax
import jax.numpy as jnp
from jax import lax
from jax.experimental import pallas as pl
from jax.experimental.pallas import tpu as pltpu

D_MODEL = 1024
CHUNK = 64
CONV_WIDTH = 31
D_CONV = 512
N_HEADS = 4
HEAD_DIM = 64
V_DIM = 128
D_ATTN = 512
D_QK = 512
EPS = 1e-6
LAMBDA_INIT = 0.8 - 0.6 * math.exp(-0.3 * 0)
LOG2E = math.log2(math.e)
SLOPES = tuple(2.0 ** (-8.0 * (h + 1) / N_HEADS) for h in range(N_HEADS))

BF16 = jnp.bfloat16
F32 = jnp.float32

VMEM_LIMIT_BYTES = 56 * 1024 * 1024

TM_IN = 1024
TS_CONV = 256
CONV_HALO = 32
CONV_ROWS = 64
TQ = 256
TK = 256
TM_OUT = 1024


def _sigmoid(x):
    return 1.0 / (1.0 + jnp.exp(-x))


def _silu(x):
    return x * _sigmoid(x)


def _inproj_kernel(x_ref, gpre_ref, w_ref, glu_ref, zc_ref, qt_ref, k_ref, vt_ref, za_ref, g_ref):
    x = x_ref[...]
    ms = jnp.mean(x * x, axis=-1, keepdims=True)
    h = (x * lax.rsqrt(ms + EPS) * gpre_ref[...]).astype(BF16)

    def proj(c0, width):
        return jnp.dot(h, w_ref[:, c0:c0 + width], preferred_element_type=F32)

    def store_transposed(t_ref, y):
        for blk in range(TM_IN // TK):
            for hd in range(N_HEADS):
                t_ref[blk, hd] = y[blk * TK:(blk + 1) * TK, hd * V_DIM:(hd + 1) * V_DIM].T.astype(BF16)

    a = proj(0, D_CONV)
    b = proj(D_CONV, D_CONV)
    glu_ref[...] = (a * _sigmoid(b)).astype(BF16)
    zc_ref[...] = _silu(proj(2 * D_CONV, D_CONV)).astype(BF16)
    c0 = 3 * D_CONV
    store_transposed(qt_ref, proj(c0, D_QK) * (LOG2E / math.sqrt(HEAD_DIM)))
    k_ref[...] = proj(c0 + D_QK, D_QK).astype(BF16)
    store_transposed(vt_ref, proj(c0 + 2 * D_QK, D_ATTN))
    za_ref[...] = _silu(proj(c0 + 2 * D_QK + D_ATTN, D_ATTN)).astype(BF16)
    c0 = c0 + 2 * D_QK + 2 * D_ATTN
    for j in range(2 * D_MODEL // 512):
        g_ref[:, j * 512:(j + 1) * 512] = _sigmoid(proj(c0 + j * 512, 512)).astype(BF16)


def _inproj(x2, gpre, w_in):
    T = x2.shape[0]
    d_in = w_in.shape[1]
    row = lambda i: (i, 0)
    fixed = lambda i: (0, 0)
    tr_shape = (T // TK, N_HEADS, V_DIM, TK)
    tr_spec = pl.BlockSpec((TM_IN // TK, N_HEADS, V_DIM, TK), lambda i: (i, 0, 0, 0))
    rows = lambda w: (pl.BlockSpec((TM_IN, w), row), jax.ShapeDtypeStruct((T, w), BF16))
    tr = (tr_spec, jax.ShapeDtypeStruct(tr_shape, BF16))
    outs = (rows(D_CONV), rows(D_CONV), tr, rows(D_QK), tr, rows(D_ATTN), rows(2 * D_MODEL))
    return pl.pallas_call(
        _inproj_kernel,
        grid=(T // TM_IN,),
        in_specs=[
            pl.BlockSpec((TM_IN, D_MODEL), row),
            pl.BlockSpec((1, D_MODEL), fixed),
            pl.BlockSpec((D_MODEL, d_in), fixed, pipeline_mode=pl.Buffered(1)),
        ],
        out_specs=[o[0] for o in outs],
        out_shape=[o[1] for o in outs],
        compiler_params=pltpu.CompilerParams(
            dimension_semantics=("arbitrary",), vmem_limit_bytes=VMEM_LIMIT_BYTES),
        name="inproj",
    )(x2, gpre, w_in)


def _conv_kernel(glu_ref, halo_ref, zc_ref, w_ref, b_ref, lng_ref, lnb_ref, o_ref, xpad_ref, acc_ref):
    i = pl.program_id(1)
    halo = jnp.where(i == 0, 0.0, halo_ref[0].astype(F32))
    cur = glu_ref[0].astype(F32)
    for c in range(D_CONV // 128):
        xpad_ref[c, 0:CONV_HALO, :] = halo[:, c * 128:(c + 1) * 128]
        xpad_ref[c, CONV_HALO:, :] = cur[:, c * 128:(c + 1) * 128]

    shift = CONV_HALO - (CONV_WIDTH - 1)

    def rows(r, carry):
        r0 = pl.multiple_of(r * CONV_ROWS, CONV_ROWS)
        for c in range(D_CONV // 128):
            acc = jnp.zeros((CONV_ROWS, 128), F32)
            for j in range(CONV_WIDTH):
                xs = xpad_ref[c, pl.ds(r0 + (shift + j), CONV_ROWS), :]
                acc = acc + xs * w_ref[j:j + 1, c * 128:(c + 1) * 128]
            acc_ref[pl.ds(r0, CONV_ROWS), c * 128:(c + 1) * 128] = acc
        return carry

    lax.fori_loop(0, TS_CONV // CONV_ROWS, rows, 0)

    y = acc_ref[...] + b_ref[...]
    mu = jnp.mean(y, axis=-1, keepdims=True)
    yc = y - mu
    var = jnp.mean(yc * yc, axis=-1, keepdims=True)
    yn = yc * lax.rsqrt(var + EPS) * lng_ref[...] + lnb_ref[...]
    o_ref[0] = (_silu(yn) * zc_ref[0].astype(F32)).astype(BF16)


def _conv(glu, zc, conv_w, conv_b, ln_g, ln_b):
    B, S, _ = glu.shape
    blocks_per_halo = TS_CONV // CONV_HALO
    cur = lambda b, i: (b, i, 0)
    prev = lambda b, i: (b, jnp.maximum(i * blocks_per_halo - 1, 0), 0)
    fixed = lambda b, i: (0, 0)
    return pl.pallas_call(
        _conv_kernel,
        grid=(B, S // TS_CONV),
        in_specs=[
            pl.BlockSpec((1, TS_CONV, D_CONV), cur),
            pl.BlockSpec((1, CONV_HALO, D_CONV), prev),
            pl.BlockSpec((1, TS_CONV, D_CONV), cur),
            pl.BlockSpec((CONV_WIDTH, D_CONV), fixed),
            pl.BlockSpec((1, D_CONV), fixed),
            pl.BlockSpec((1, D_CONV), fixed),
            pl.BlockSpec((1, D_CONV), fixed),
        ],
        out_specs=pl.BlockSpec((1, TS_CONV, D_CONV), cur),
        out_shape=jax.ShapeDtypeStruct((B, S, D_CONV), BF16),
        scratch_shapes=[
            pltpu.VMEM((D_CONV // 128, TS_CONV + CONV_HALO, 128), F32),
            pltpu.VMEM((TS_CONV, D_CONV), F32),
        ],
        compiler_params=pltpu.CompilerParams(
            dimension_semantics=("arbitrary", "arbitrary"), vmem_limit_bytes=VMEM_LIMIT_BYTES),
        name="conv",
    )(glu, glu, zc, conv_w, conv_b, ln_g, ln_b)


def _bf16_round(x):
    bits = struct.unpack("<I", struct.pack("<f", x))[0]
    bits = (bits + 0x7FFF + ((bits >> 16) & 1)) & 0xFFFF0000
    return struct.unpack("<f", struct.pack("<I", bits))[0]


def _bf16_split3(c):
    c1 = _bf16_round(c)
    c2 = _bf16_round(c - c1)
    c3 = _bf16_round(c - c1 - c2)
    return c1, c2, c3


def _attn_kernel(qt_ref, k_ref, vt_ref, za_ref, lq1_ref, lk1_ref, lq2_ref, lk2_ref, subg_ref,
                 o_ref, qaug_ref, e_ref, tile_ref, acc_ref, m_ref, alpha_ref, s_ref, smax_ref, p_ref):
    i = pl.program_id(1)
    t0 = i * TQ
    slopes2 = [s * LOG2E for s in SLOPES]

    @pl.when((pl.program_id(0) == 0) & (i == 0))
    def _():
        sl = lax.broadcasted_iota(jnp.int32, (TK, 2 * TQ), 0)
        tl = lax.broadcasted_iota(jnp.int32, (TK, 2 * TQ), 1) % TQ
        allowed = (sl // CHUNK) <= (tl // CHUNK)
        rel = (tl - jnp.abs(tl - sl) - sl).astype(F32)
        crow = lax.broadcasted_iota(jnp.int32, (2 * HEAD_DIM, 2 * TQ), 0)
        for h in range(N_HEADS):
            tile_ref[h] = jnp.where(allowed, slopes2[h] * rel, -jnp.inf)
            c1, c2, c3 = _bf16_split3(slopes2[h])
            consts = jnp.where(crow == 0, c1, jnp.where(crow == 1, c2, jnp.where(crow == 2, c3, 0.0)))
            qaug_ref[h, 2 * HEAD_DIM:, :] = consts.astype(BF16)
        lane = lax.broadcasted_iota(jnp.int32, (TK, 2 * HEAD_DIM), 1)
        row = lax.broadcasted_iota(jnp.int32, (TK, 2 * HEAD_DIM), 0)
        e_ref[...] = jnp.where(lane < 3, row, 0).astype(F32).astype(BF16)

    zeros = jnp.zeros((HEAD_DIM, TQ), BF16)
    for h in range(N_HEADS):
        qt = qt_ref[0, h]
        qaug_ref[h, 0:2 * HEAD_DIM, :] = jnp.concatenate(
            [jnp.concatenate([qt[:HEAD_DIM], zeros], axis=0),
             jnp.concatenate([zeros, qt[HEAD_DIM:]], axis=0)], axis=1)

    ones = jnp.ones((16, TK), BF16)

    def scores(h, j):
        j0 = pl.multiple_of(j * TK, TK)
        lhs = jnp.concatenate([k_ref[0, pl.ds(j0, TK), h * V_DIM:(h + 1) * V_DIM], e_ref[...]], axis=1)
        return jnp.dot(lhs, qaug_ref[h], preferred_element_type=F32)

    def colmax(s):
        return jnp.max(s, axis=0, keepdims=True)

    def softmax(h, j, first):
        if first:
            m_new = smax_ref[h]
            base = m_new
            alpha_ref[h] = jnp.zeros((1, 2 * TQ), F32)
        else:
            shift = slopes2[h] * jnp.full((1, 2 * TQ), j * TK - t0, jnp.int32).astype(F32)
            m_old = m_ref[h]
            m_new = jnp.maximum(m_old, smax_ref[h] + shift)
            base = m_new - shift
            alpha_ref[h] = jnp.exp2(m_old - m_new)
        p_ref[h] = jnp.exp2(s_ref[h] - base).astype(BF16)
        m_ref[h] = m_new

    def accumulate(h, j):
        lhs_v = jnp.concatenate([vt_ref[j, h], ones], axis=0)
        pv = jnp.dot(lhs_v, p_ref[h], preferred_element_type=F32)
        acc_ref[h] = acc_ref[h] * alpha_ref[h] + pv

    for h in range(N_HEADS):
        acc_ref[h] = jnp.zeros((V_DIM + 16, 2 * TQ), F32)
        s_cur = scores(h, i) + tile_ref[h]
        s_ref[h] = s_cur
        smax_ref[h] = colmax(s_cur)
        s_nxt = scores(h, 0)
        m_nxt = colmax(s_nxt)
        softmax(h, i, True)
        s_ref[h] = s_nxt
        smax_ref[h] = m_nxt

    def body(t, carry):
        prev = jnp.where(t == 0, i, t - 1)
        nxt = jnp.minimum(t + 1, i)
        for h in range(N_HEADS):
            s_nxt = scores(h, nxt)
            m_nxt = colmax(s_nxt)
            accumulate(h, prev)
            softmax(h, t, False)
            s_ref[h] = s_nxt
            smax_ref[h] = m_nxt
        return carry

    lax.fori_loop(0, i, body, 0)
    lam = (jnp.exp(jnp.sum(lq1_ref[...] * lk1_ref[...], axis=-1, keepdims=True))
           - jnp.exp(jnp.sum(lq2_ref[...] * lk2_ref[...], axis=-1, keepdims=True))
           + LAMBDA_INIT)
    for h in range(N_HEADS):
        accumulate(h, jnp.where(i == 0, i, i - 1))
        hs = slice(h * V_DIM, (h + 1) * V_DIM)
        a = acc_ref[h]
        r = 1.0 / a[V_DIM:V_DIM + 1, :]
        ot = a[:V_DIM, :TQ] * r[:, :TQ] - lam * (a[:V_DIM, TQ:] * r[:, TQ:])
        ot = ot * lax.rsqrt(jnp.mean(ot * ot, axis=0, keepdims=True) + EPS)
        o = ot.T * (subg_ref[...] * (1.0 - LAMBDA_INIT))
        o_ref[0, :, hs] = (o * za_ref[0, :, hs].astype(F32)).astype(BF16)


def _attn(qt, k, vt, za, lq1, lk1, lq2, lk2, subg):
    B, S, _ = k.shape
    nq = S // TQ
    blk = lambda b, i: (b, i, 0)
    whole = lambda b, i: (b, 0, 0)
    fixed = lambda b, i: (0, 0)
    return pl.pallas_call(
        _attn_kernel,
        grid=(B, nq),
        in_specs=[
            pl.BlockSpec((1, N_HEADS, 2 * HEAD_DIM, TQ), lambda b, i: (b * nq + i, 0, 0, 0)),
            pl.BlockSpec((1, S, D_QK), whole),
            pl.BlockSpec((S // TK, N_HEADS, V_DIM, TK), lambda b, i: (b, 0, 0, 0)),
            pl.BlockSpec((1, TQ, D_ATTN), blk),
            pl.BlockSpec((1, HEAD_DIM), fixed),
            pl.BlockSpec((1, HEAD_DIM), fixed),
            pl.BlockSpec((1, HEAD_DIM), fixed),
            pl.BlockSpec((1, HEAD_DIM), fixed),
            pl.BlockSpec((1, V_DIM), fixed),
        ],
        out_specs=pl.BlockSpec((1, TQ, D_ATTN), blk),
        out_shape=jax.ShapeDtypeStruct((B, S, D_ATTN), BF16),
        scratch_shapes=[
            pltpu.VMEM((N_HEADS, 4 * HEAD_DIM, 2 * TQ), BF16),
            pltpu.VMEM((TK, 2 * HEAD_DIM), BF16),
            pltpu.VMEM((N_HEADS, TK, 2 * TQ), F32),
            pltpu.VMEM((N_HEADS, V_DIM + 16, 2 * TQ), F32),
            pltpu.VMEM((N_HEADS, 1, 2 * TQ), F32),
            pltpu.VMEM((N_HEADS, 1, 2 * TQ), F32),
            pltpu.VMEM((N_HEADS, TK, 2 * TQ), F32),
            pltpu.VMEM((N_HEADS, 1, 2 * TQ), F32),
            pltpu.VMEM((N_HEADS, TK, 2 * TQ), BF16),
        ],
        compiler_params=pltpu.CompilerParams(
            dimension_semantics=("arbitrary", "arbitrary"), vmem_limit_bytes=VMEM_LIMIT_BYTES),
        name="attn",
    )(qt, k, vt, za, lq1, lk1, lq2, lk2, subg)


def _out_kernel(x_ref, hc_ref, ha_ref, g_ref, wc_ref, wa_ref, wo_ref, gpost_ref, o_ref):
    yc = jnp.dot(hc_ref[...], wc_ref[...], preferred_element_type=F32)
    ya = jnp.dot(ha_ref[...], wa_ref[...], preferred_element_type=F32)
    mixed = (g_ref[:, :D_MODEL].astype(F32) * yc + g_ref[:, D_MODEL:].astype(F32) * ya).astype(BF16)
    out = jnp.dot(mixed, wo_ref[...], preferred_element_type=F32)
    ms = jnp.mean(out * out, axis=-1, keepdims=True)
    o_ref[...] = x_ref[...] + out * lax.rsqrt(ms + EPS) * gpost_ref[...]


def _out(x2, hc, ha, g, wc, wa, wo, gpost):
    T = x2.shape[0]
    row = lambda i: (i, 0)
    fixed = lambda i: (0, 0)
    return pl.pallas_call(
        _out_kernel,
        grid=(T // TM_OUT,),
        in_specs=[
            pl.BlockSpec((TM_OUT, D_MODEL), row),
            pl.BlockSpec((TM_OUT, D_CONV), row),
            pl.BlockSpec((TM_OUT, D_ATTN), row),
            pl.BlockSpec((TM_OUT, 2 * D_MODEL), row),
            pl.BlockSpec((D_CONV, D_MODEL), fixed),
            pl.BlockSpec((D_ATTN, D_MODEL), fixed),
            pl.BlockSpec((D_MODEL, D_MODEL), fixed),
            pl.BlockSpec((1, D_MODEL), fixed),
        ],
        out_specs=pl.BlockSpec((TM_OUT, D_MODEL), row),
        out_shape=jax.ShapeDtypeStruct((T, D_MODEL), F32),
        compiler_params=pltpu.CompilerParams(
            dimension_semantics=("arbitrary",), vmem_limit_bytes=VMEM_LIMIT_BYTES),
        name="out",
    )(x2, hc, ha, g, wc, wa, wo, gpost)


def kernel(x, w_in, conv_w, conv_b, conv_ln_g, conv_ln_b, w_conv_proj, lambda_q1, lambda_k1,
           lambda_q2, lambda_k2, subln_g, w_attn_proj, w_out, norm_pre_g, norm_post_g):
    B, S, D = x.shape
    assert (D, w_in.shape[0]) == (D_MODEL, 1) and TQ == TK
    assert S % max(TQ, TS_CONV) == 0 and (B * S) % TM_IN == 0
    T = B * S
    x2 = x.reshape(T, D)
    glu, zc, qt, k, vt, za, g = _inproj(x2, norm_pre_g, w_in[0].astype(BF16))
    seq = lambda a: a.reshape(B, S, a.shape[-1])
    hc = _conv(seq(glu), seq(zc), conv_w[0], conv_b, conv_ln_g, conv_ln_b)
    ha = _attn(qt, seq(k), vt, seq(za), lambda_q1, lambda_k1, lambda_q2, lambda_k2, subln_g)
    y = _out(x2, hc.reshape(T, D_CONV), ha.reshape(T, D_ATTN), g,
             w_conv_proj[0].astype(BF16), w_attn_proj[0].astype(BF16), w_out[0].astype(BF16),
             norm_post_g)
    return y.reshape(B, S, D)
```

```python
import math
import struct

import jax
import jax.numpy as jnp
from jax import lax
from jax.experimental import pallas as pl
from jax.experimental.pallas import tpu as pltpu

D_MODEL = 1024
CHUNK = 64
CONV_WIDTH = 31
D_CONV = 512
N_HEADS = 4
HEAD_DIM = 64
V_DIM = 128
D_ATTN = 512
D_QK = 512
EPS = 1e-6
LAMBDA_INIT = 0.8 - 0.6 * math.exp(-0.3 * 0)
LOG2E = math.log2(math.e)
SLOPES = tuple(2.0 ** (-8.0 * (h + 1) / N_HEADS) for h in range(N_HEADS))

BF16 = jnp.bfloat16
F32 = jnp.float32

VMEM_LIMIT_BYTES = 56 * 1024 * 1024

TM_IN = 1024
TS_CONV = 1024
CONV_HALO = 32
CONV_ROWS = 256
TQ = 256
TK = 256
TM_OUT = 1024


def _sigmoid(x):
    return 1.0 / (1.0 + jnp.exp(-x))


def _silu(x):
    return x * _sigmoid(x)


def _inproj_kernel(x_ref, gpre_ref, w_ref, glu_ref, zc_ref, qt_ref, k_ref, vt_ref, za_ref, g_ref):
    x = x_ref[...]
    ms = jnp.mean(x * x, axis=-1, keepdims=True)
    h = (x * lax.rsqrt(ms + EPS) * gpre_ref[...]).astype(BF16)

    def proj(c0, width):
        return jnp.dot(h, w_ref[:, c0:c0 + width], preferred_element_type=F32)

    def store_transposed(t_ref, y):
        for blk in range(TM_IN // TK):
            for hd in range(N_HEADS):
                t_ref[blk, hd] = y[blk * TK:(blk + 1) * TK, hd * V_DIM:(hd + 1) * V_DIM].T.astype(BF16)

    a = proj(0, D_CONV)
    b = proj(D_CONV, D_CONV)
    glu_ref[...] = (a * _sigmoid(b)).astype(BF16)
    zc_ref[...] = _silu(proj(2 * D_CONV, D_CONV)).astype(BF16)
    c0 = 3 * D_CONV
    store_transposed(qt_ref, proj(c0, D_QK) * (LOG2E / math.sqrt(HEAD_DIM)))
    k_ref[...] = proj(c0 + D_QK, D_QK).astype(BF16)
    store_transposed(vt_ref, proj(c0 + 2 * D_QK, D_ATTN))
    za_ref[...] = _silu(proj(c0 + 2 * D_QK + D_ATTN, D_ATTN)).astype(BF16)
    c0 = c0 + 2 * D_QK + 2 * D_ATTN
    for j in range(2 * D_MODEL // 512):
        g_ref[:, j * 512:(j + 1) * 512] = _sigmoid(proj(c0 + j * 512, 512)).astype(BF16)


def _inproj(x2, gpre, w_in):
    T = x2.shape[0]
    d_in = w_in.shape[1]
    row = lambda i: (i, 0)
    fixed = lambda i: (0, 0)
    tr_shape = (T // TK, N_HEADS, V_DIM, TK)
    tr_spec = pl.BlockSpec((TM_IN // TK, N_HEADS, V_DIM, TK), lambda i: (i, 0, 0, 0))
    rows = lambda w: (pl.BlockSpec((TM_IN, w), row), jax.ShapeDtypeStruct((T, w), BF16))
    tr = (tr_spec, jax.ShapeDtypeStruct(tr_shape, BF16))
    outs = (rows(D_CONV), rows(D_CONV), tr, rows(D_QK), tr, rows(D_ATTN), rows(2 * D_MODEL))
    return pl.pallas_call(
        _inproj_kernel,
        grid=(T // TM_IN,),
        in_specs=[
            pl.BlockSpec((TM_IN, D_MODEL), row),
            pl.BlockSpec((1, D_MODEL), fixed),
            pl.BlockSpec((D_MODEL, d_in), fixed, pipeline_mode=pl.Buffered(1)),
        ],
        out_specs=[o[0] for o in outs],
        out_shape=[o[1] for o in outs],
        compiler_params=pltpu.CompilerParams(
            dimension_semantics=("arbitrary",), vmem_limit_bytes=VMEM_LIMIT_BYTES),
        name="inproj",
    )(x2, gpre, w_in)


def _conv_kernel(glu_ref, halo_ref, zc_ref, w_ref, b_ref, lng_ref, lnb_ref, o_ref, xpad_ref, acc_ref):
    i = pl.program_id(1)
    halo = jnp.where(i == 0, 0.0, halo_ref[0].astype(F32))
    cur = glu_ref[0].astype(F32)
    for c in range(D_CONV // 128):
        xpad_ref[c, 0:CONV_HALO, :] = halo[:, c * 128:(c + 1) * 128]
        xpad_ref[c, CONV_HALO:, :] = cur[:, c * 128:(c + 1) * 128]

    shift = CONV_HALO - (CONV_WIDTH - 1)

    def rows(r, carry):
        r0 = pl.multiple_of(r * CONV_ROWS, CONV_ROWS)
        for c in range(D_CONV // 128):
            acc = jnp.zeros((CONV_ROWS, 128), F32)
            for j in range(CONV_WIDTH):
                xs = xpad_ref[c, pl.ds(r0 + (shift + j), CONV_ROWS), :]
                acc = acc + xs * w_ref[j:j + 1, c * 128:(c + 1) * 128]
            acc_ref[pl.ds(r0, CONV_ROWS), c * 128:(c + 1) * 128] = acc
        return carry

    lax.fori_loop(0, TS_CONV // CONV_ROWS, rows, 0)

    y = acc_ref[...] + b_ref[...]
    mu = jnp.mean(y, axis=-1, keepdims=True)
    yc = y - mu
    var = jnp.mean(yc * yc, axis=-1, keepdims=True)
    yn = yc * lax.rsqrt(var + EPS) * lng_ref[...] + lnb_ref[...]
    o_ref[0] = (_silu(yn) * zc_ref[0].astype(F32)).astype(BF16)


def _conv(glu, zc, conv_w, conv_b, ln_g, ln_b):
    B, S, _ = glu.shape
    blocks_per_halo = TS_CONV // CONV_HALO
    cur = lambda b, i: (b, i, 0)
    prev = lambda b, i: (b, jnp.maximum(i * blocks_per_halo - 1, 0), 0)
    fixed = lambda b, i: (0, 0)
    return pl.pallas_call(
        _conv_kernel,
        grid=(B, S // TS_CONV),
        in_specs=[
            pl.BlockSpec((1, TS_CONV, D_CONV), cur),
            pl.BlockSpec((1, CONV_HALO, D_CONV), prev),
            pl.BlockSpec((1, TS_CONV, D_CONV), cur),
            pl.BlockSpec((CONV_WIDTH, D_CONV), fixed),
            pl.BlockSpec((1, D_CONV), fixed),
            pl.BlockSpec((1, D_CONV), fixed),
            pl.BlockSpec((1, D_CONV), fixed),
        ],
        out_specs=pl.BlockSpec((1, TS_CONV, D_CONV), cur),
        out_shape=jax.ShapeDtypeStruct((B, S, D_CONV), BF16),
        scratch_shapes=[
            pltpu.VMEM((D_CONV // 128, TS_CONV + CONV_HALO, 128), F32),
            pltpu.VMEM((TS_CONV, D_CONV), F32),
        ],
        compiler_params=pltpu.CompilerParams(
            dimension_semantics=("arbitrary", "arbitrary"), vmem_limit_bytes=VMEM_LIMIT_BYTES),
        name="conv",
    )(glu, glu, zc, conv_w, conv_b, ln_g, ln_b)


def _bf16_round(x):
    bits = struct.unpack("<I", struct.pack("<f", x))[0]
    bits = (bits + 0x7FFF + ((bits >> 16) & 1)) & 0xFFFF0000
    return struct.unpack("<f", struct.pack("<I", bits))[0]


def _bf16_split3(c):
    c1 = _bf16_round(c)
    c2 = _bf16_round(c - c1)
    c3 = _bf16_round(c - c1 - c2)
    return c1, c2, c3


def _attn_kernel(qt_ref, k_ref, vt_ref, za_ref, lq1_ref, lk1_ref, lq2_ref, lk2_ref, subg_ref,
                 o_ref, qaug_ref, e_ref, tile_ref, acc_ref, m_ref, alpha_ref, s_ref, smax_ref, p_ref):
    i = pl.program_id(1)
    t0 = i * TQ
    slopes2 = [s * LOG2E for s in SLOPES]

    @pl.when((pl.program_id(0) == 0) & (i == 0))
    def _():
        sl = lax.broadcasted_iota(jnp.int32, (TK, 2 * TQ), 0)
        tl = lax.broadcasted_iota(jnp.int32, (TK, 2 * TQ), 1) % TQ
        allowed = (sl // CHUNK) <= (tl // CHUNK)
        rel = (tl - jnp.abs(tl - sl) - sl).astype(F32)
        crow = lax.broadcasted_iota(jnp.int32, (2 * HEAD_DIM, 2 * TQ), 0)
        for h in range(N_HEADS):
            tile_ref[h] = jnp.where(allowed, slopes2[h] * rel, -jnp.inf)
            c1, c2, c3 = _bf16_split3(slopes2[h])
            consts = jnp.where(crow == 0, c1, jnp.where(crow == 1, c2, jnp.where(crow == 2, c3, 0.0)))
            qaug_ref[h, 2 * HEAD_DIM:, :] = consts.astype(BF16)
        lane = lax.broadcasted_iota(jnp.int32, (TK, 2 * HEAD_DIM), 1)
        row = lax.broadcasted_iota(jnp.int32, (TK, 2 * HEAD_DIM), 0)
        e_ref[...] = jnp.where(lane < 3, row, 0).astype(F32).astype(BF16)

    zeros = jnp.zeros((HEAD_DIM, TQ), BF16)
    for h in range(N_HEADS):
        qt = qt_ref[0, h]
        qaug_ref[h, 0:2 * HEAD_DIM, :] = jnp.concatenate(
            [jnp.concatenate([qt[:HEAD_DIM], zeros], axis=0),
             jnp.concatenate([zeros, qt[HEAD_DIM:]], axis=0)], axis=1)

    ones = jnp.ones((16, TK), BF16)

    def scores(h, j):
        j0 = pl.multiple_of(j * TK, TK)
        lhs = jnp.concatenate([k_ref[0, pl.ds(j0, TK), h * V_DIM:(h + 1) * V_DIM], e_ref[...]], axis=1)
        return jnp.dot(lhs, qaug_ref[h], preferred_element_type=F32)

    def colmax(s):
        return jnp.max(s, axis=0, keepdims=True)

    def softmax(h, j, first):
        if first:
            m_new = smax_ref[h]
            base = m_new
            alpha_ref[h] = jnp.zeros((1, 2 * TQ), F32)
        else:
            shift = slopes2[h] * jnp.full((1, 2 * TQ), j * TK - t0, jnp.int32).astype(F32)
            m_old = m_ref[h]
            m_new = jnp.maximum(m_old, smax_ref[h] + shift)
            base = m_new - shift
            alpha_ref[h] = jnp.exp2(m_old - m_new)
        p_ref[h] = jnp.exp2(s_ref[h] - base).astype(BF16)
        m_ref[h] = m_new

    def accumulate(h, j):
        lhs_v = jnp.concatenate([vt_ref[j, h], ones], axis=0)
        pv = jnp.dot(lhs_v, p_ref[h], preferred_element_type=F32)
        acc_ref[h] = acc_ref[h] * alpha_ref[h] + pv

    for h in range(N_HEADS):
        acc_ref[h] = jnp.zeros((V_DIM + 16, 2 * TQ), F32)
        s_cur = scores(h, i) + tile_ref[h]
        s_ref[h] = s_cur
        smax_ref[h] = colmax(s_cur)
        s_nxt = scores(h, 0)
        m_nxt = colmax(s_nxt)
        softmax(h, i, True)
        s_ref[h] = s_nxt
        smax_ref[h] = m_nxt

    def body(t, carry):
        prev = jnp.where(t == 0, i, t - 1)
        nxt = jnp.minimum(t + 1, i)
        for h in range(N_HEADS):
            s_nxt = scores(h, nxt)
            m_nxt = colmax(s_nxt)
            accumulate(h, prev)
            softmax(h, t, False)
            s_ref[h] = s_nxt
            smax_ref[h] = m_nxt
        return carry

    lax.fori_loop(0, i, body, 0)
    lam = (jnp.exp(jnp.sum(lq1_ref[...] * lk1_ref[...], axis=-1, keepdims=True))
           - jnp.exp(jnp.sum(lq2_ref[...] * lk2_ref[...], axis=-1, keepdims=True))
           + LAMBDA_INIT)
    for h in range(N_HEADS):
        accumulate(h, jnp.where(i == 0, i, i - 1))
        hs = slice(h * V_DIM, (h + 1) * V_DIM)
        a = acc_ref[h]
        r = 1.0 / a[V_DIM:V_DIM + 1, :]
        ot = a[:V_DIM, :TQ] * r[:, :TQ] - lam * (a[:V_DIM, TQ:] * r[:, TQ:])
        ot = ot * lax.rsqrt(jnp.mean(ot * ot, axis=0, keepdims=True) + EPS)
        o = ot.T * (subg_ref[...] * (1.0 - LAMBDA_INIT))
        o_ref[0, :, hs] = (o * za_ref[0, :, hs].astype(F32)).astype(BF16)


def _attn(qt, k, vt, za, lq1, lk1, lq2, lk2, subg):
    B, S, _ = k.shape
    nq = S // TQ
    blk = lambda b, i: (b, i, 0)
    whole = lambda b, i: (b, 0, 0)
    fixed = lambda b, i: (0, 0)
    return pl.pallas_call(
        _attn_kernel,
        grid=(B, nq),
        in_specs=[
            pl.BlockSpec((1, N_HEADS, 2 * HEAD_DIM, TQ), lambda b, i: (b * nq + i, 0, 0, 0)),
            pl.BlockSpec((1, S, D_QK), whole),
            pl.BlockSpec((S // TK, N_HEADS, V_DIM, TK), lambda b, i: (b, 0, 0, 0)),
            pl.BlockSpec((1, TQ, D_ATTN), blk),
            pl.BlockSpec((1, HEAD_DIM), fixed),
            pl.BlockSpec((1, HEAD_DIM), fixed),
            pl.BlockSpec((1, HEAD_DIM), fixed),
            pl.BlockSpec((1, HEAD_DIM), fixed),
            pl.BlockSpec((1, V_DIM), fixed),
        ],
        out_specs=pl.BlockSpec((1, TQ, D_ATTN), blk),
        out_shape=jax.ShapeDtypeStruct((B, S, D_ATTN), BF16),
        scratch_shapes=[
            pltpu.VMEM((N_HEADS, 4 * HEAD_DIM, 2 * TQ), BF16),
            pltpu.VMEM((TK, 2 * HEAD_DIM), BF16),
            pltpu.VMEM((N_HEADS, TK, 2 * TQ), F32),
            pltpu.VMEM((N_HEADS, V_DIM + 16, 2 * TQ), F32),
            pltpu.VMEM((N_HEADS, 1, 2 * TQ), F32),
            pltpu.VMEM((N_HEADS, 1, 2 * TQ), F32),
            pltpu.VMEM((N_HEADS, TK, 2 * TQ), F32),
            pltpu.VMEM((N_HEADS, 1, 2 * TQ), F32),
            pltpu.VMEM((N_HEADS, TK, 2 * TQ), BF16),
        ],
        compiler_params=pltpu.CompilerParams(
            dimension_semantics=("arbitrary", "arbitrary"), vmem_limit_bytes=VMEM_LIMIT_BYTES),
        name="attn",
    )(qt, k, vt, za, lq1, lk1, lq2, lk2, subg)


def _out_kernel(x_ref, hc_ref, ha_ref, g_ref, wc_ref, wa_ref, wo_ref, gpost_ref, o_ref):
    yc = jnp.dot(hc_ref[...], wc_ref[...], preferred_element_type=F32)
    ya = jnp.dot(ha_ref[...], wa_ref[...], preferred_element_type=F32)
    mixed = (g_ref[:, :D_MODEL].astype(F32) * yc + g_ref[:, D_MODEL:].astype(F32) * ya).astype(BF16)
    out = jnp.dot(mixed, wo_ref[...], preferred_element_type=F32)
    ms = jnp.mean(out * out, axis=-1, keepdims=True)
    o_ref[...] = x_ref[...] + out * lax.rsqrt(ms + EPS) * gpost_ref[...]


def _out(x2, hc, ha, g, wc, wa, wo, gpost):
    T = x2.shape[0]
    row = lambda i: (i, 0)
    fixed = lambda i: (0, 0)
    return pl.pallas_call(
        _out_kernel,
        grid=(T // TM_OUT,),
        in_specs=[
            pl.BlockSpec((TM_OUT, D_MODEL), row),
            pl.BlockSpec((TM_OUT, D_CONV), row),
            pl.BlockSpec((TM_OUT, D_ATTN), row),
            pl.BlockSpec((TM_OUT, 2 * D_MODEL), row),
            pl.BlockSpec((D_CONV, D_MODEL), fixed),
            pl.BlockSpec((D_ATTN, D_MODEL), fixed),
            pl.BlockSpec((D_MODEL, D_MODEL), fixed),
            pl.BlockSpec((1, D_MODEL), fixed),
        ],
        out_specs=pl.BlockSpec((TM_OUT, D_MODEL), row),
        out_shape=jax.ShapeDtypeStruct((T, D_MODEL), F32),
        compiler_params=pltpu.CompilerParams(
            dimension_semantics=("arbitrary",), vmem_limit_bytes=VMEM_LIMIT_BYTES),
        name="out",
    )(x2, hc, ha, g, wc, wa, wo, gpost)


def kernel(x, w_in, conv_w, conv_b, conv_ln_g, conv_ln_b, w_conv_proj, lambda_q1, lambda_k1,
           lambda_q2, lambda_k2, subln_g, w_attn_proj, w_out, norm_pre_g, norm_post_g):
    B, S, D = x.shape
    assert (D, w_in.shape[0]) == (D_MODEL, 1) and TQ == TK
    assert S % max(TQ, TS_CONV) == 0 and (B * S) % TM_IN == 0
    T = B * S
    x2 = x.reshape(T, D)
    glu, zc, qt, k, vt, za, g = _inproj(x2, norm_pre_g, w_in[0].astype(BF16))
    seq = lambda a: a.reshape(B, S, a.shape[-1])
    hc = _conv(seq(glu), seq(zc), conv_w[0], conv_b, conv_ln_g, conv_ln_b)
    ha = _attn(qt, seq(k), vt, seq(za), lambda_q1, lambda_k1, lambda_q2, lambda_k2, subln_g)
    y = _out(x2, hc.reshape(T, D_CONV), ha.reshape(T, D_ATTN), g,
             w_conv_proj[0].astype(BF16), w_attn_proj[0].astype(BF16), w_out[0].astype(BF16),
             norm_post_g)
    return y.reshape(B, S, D)
```

```python
import math
import struct

import jax
import jax.numpy as jnp
from jax import lax
from jax.experimental import pallas as pl
from jax.experimental.pallas import tpu as pltpu

D_MODEL = 1024
CHUNK = 64
CONV_WIDTH = 31
D_CONV = 512
N_HEADS = 4
HEAD_DIM = 64
V_DIM = 128
D_ATTN = 512
D_QK = 512
EPS = 1e-6
LAMBDA_INIT = 0.8 - 0.6 * math.exp(-0.3 * 0)
LOG2E = math.log2(math.e)
SLOPES = tuple(2.0 ** (-8.0 * (h + 1) / N_HEADS) for h in range(N_HEADS))

BF16 = jnp.bfloat16
F32 = jnp.float32

VMEM_LIMIT_BYTES = 56 * 1024 * 1024

TM_IN = 1024
TS_CONV = 1024
CONV_HALO = 32
CONV_ROWS = 256
TQ = 256
TK = 256
TM_OUT = 1024


def _sigmoid(x):
    return 1.0 / (1.0 + jnp.exp(-x))


def _silu(x):
    return x * _sigmoid(x)


def _inproj_kernel(x_ref, gpre_ref, w_ref, glu_ref, zc_ref, qt_ref, k_ref, vt_ref, za_ref, g_ref):
    x = x_ref[...]
    ms = jnp.mean(x * x, axis=-1, keepdims=True)
    h = (x * lax.rsqrt(ms + EPS) * gpre_ref[...]).astype(BF16)

    def proj(c0, width):
        return jnp.dot(h, w_ref[:, c0:c0 + width], preferred_element_type=F32)

    def store_transposed(t_ref, y):
        for blk in range(TM_IN // TK):
            for hd in range(N_HEADS):
                t_ref[blk, hd] = y[blk * TK:(blk + 1) * TK, hd * V_DIM:(hd + 1) * V_DIM].T.astype(BF16)

    a = proj(0, D_CONV)
    b = proj(D_CONV, D_CONV)
    glu_ref[...] = (a * _sigmoid(b)).astype(BF16)
    zc_ref[...] = _silu(proj(2 * D_CONV, D_CONV)).astype(BF16)
    c0 = 3 * D_CONV
    store_transposed(qt_ref, proj(c0, D_QK) * (LOG2E / math.sqrt(HEAD_DIM)))
    store_transposed(vt_ref, proj(c0 + 2 * D_QK, D_ATTN))
    za_ref[...] = _silu(proj(c0 + 2 * D_QK + D_ATTN, D_ATTN)).astype(BF16)
    cg = c0 + 2 * D_QK + 2 * D_ATTN
    for j in range(2 * D_MODEL // 512):
        g_ref[:, j * 512:(j + 1) * 512] = _sigmoid(proj(cg + j * 512, 512)).astype(BF16)
    k_ref[...] = proj(c0 + D_QK, D_QK).astype(BF16)


def _inproj(x2, gpre, w_in):
    T = x2.shape[0]
    d_in = w_in.shape[1]
    row = lambda i: (i, 0)
    fixed = lambda i: (0, 0)
    tr_shape = (T // TK, N_HEADS, V_DIM, TK)
    tr_spec = pl.BlockSpec((TM_IN // TK, N_HEADS, V_DIM, TK), lambda i: (i, 0, 0, 0))
    rows = lambda w: (pl.BlockSpec((TM_IN, w), row), jax.ShapeDtypeStruct((T, w), BF16))
    tr = (tr_spec, jax.ShapeDtypeStruct(tr_shape, BF16))
    outs = (rows(D_CONV), rows(D_CONV), tr, rows(D_QK), tr, rows(D_ATTN), rows(2 * D_MODEL))
    return pl.pallas_call(
        _inproj_kernel,
        grid=(T // TM_IN,),
        in_specs=[
            pl.BlockSpec((TM_IN, D_MODEL), row),
            pl.BlockSpec((1, D_MODEL), fixed),
            pl.BlockSpec((D_MODEL, d_in), fixed, pipeline_mode=pl.Buffered(1)),
        ],
        out_specs=[o[0] for o in outs],
        out_shape=[o[1] for o in outs],
        compiler_params=pltpu.CompilerParams(
            dimension_semantics=("arbitrary",), vmem_limit_bytes=VMEM_LIMIT_BYTES),
        name="inproj",
    )(x2, gpre, w_in)


def _conv_kernel(glu_ref, halo_ref, zc_ref, w_ref, b_ref, lng_ref, lnb_ref, o_ref, xpad_ref, acc_ref):
    i = pl.program_id(1)
    halo = jnp.where(i == 0, 0.0, halo_ref[0].astype(F32))
    cur = glu_ref[0].astype(F32)
    for c in range(D_CONV // 128):
        xpad_ref[c, 0:CONV_HALO, :] = halo[:, c * 128:(c + 1) * 128]
        xpad_ref[c, CONV_HALO:, :] = cur[:, c * 128:(c + 1) * 128]

    shift = CONV_HALO - (CONV_WIDTH - 1)

    def rows(r, carry):
        r0 = pl.multiple_of(r * CONV_ROWS, CONV_ROWS)
        for c in range(D_CONV // 128):
            acc = jnp.zeros((CONV_ROWS, 128), F32)
            for j in range(CONV_WIDTH):
                xs = xpad_ref[c, pl.ds(r0 + (shift + j), CONV_ROWS), :]
                acc = acc + xs * w_ref[j:j + 1, c * 128:(c + 1) * 128]
            acc_ref[pl.ds(r0, CONV_ROWS), c * 128:(c + 1) * 128] = acc
        return carry

    lax.fori_loop(0, TS_CONV // CONV_ROWS, rows, 0)

    y = acc_ref[...] + b_ref[...]
    mu = jnp.mean(y, axis=-1, keepdims=True)
    yc = y - mu
    var = jnp.mean(yc * yc, axis=-1, keepdims=True)
    yn = yc * lax.rsqrt(var + EPS) * lng_ref[...] + lnb_ref[...]
    o_ref[0] = (_silu(yn) * zc_ref[0].astype(F32)).astype(BF16)


def _conv(glu, zc, conv_w, conv_b, ln_g, ln_b):
    B, S, _ = glu.shape
    blocks_per_halo = TS_CONV // CONV_HALO
    cur = lambda b, i: (b, i, 0)
    prev = lambda b, i: (b, jnp.maximum(i * blocks_per_halo - 1, 0), 0)
    fixed = lambda b, i: (0, 0)
    return pl.pallas_call(
        _conv_kernel,
        grid=(B, S // TS_CONV),
        in_specs=[
            pl.BlockSpec((1, TS_CONV, D_CONV), cur),
            pl.BlockSpec((1, CONV_HALO, D_CONV), prev),
            pl.BlockSpec((1, TS_CONV, D_CONV), cur),
            pl.BlockSpec((CONV_WIDTH, D_CONV), fixed),
            pl.BlockSpec((1, D_CONV), fixed),
            pl.BlockSpec((1, D_CONV), fixed),
            pl.BlockSpec((1, D_CONV), fixed),
        ],
        out_specs=pl.BlockSpec((1, TS_CONV, D_CONV), cur),
        out_shape=jax.ShapeDtypeStruct((B, S, D_CONV), BF16),
        scratch_shapes=[
            pltpu.VMEM((D_CONV // 128, TS_CONV + CONV_HALO, 128), F32),
            pltpu.VMEM((TS_CONV, D_CONV), F32),
        ],
        compiler_params=pltpu.CompilerParams(
            dimension_semantics=("arbitrary", "arbitrary"), vmem_limit_bytes=VMEM_LIMIT_BYTES),
        name="conv",
    )(glu, glu, zc, conv_w, conv_b, ln_g, ln_b)


def _bf16_round(x):
    bits = struct.unpack("<I", struct.pack("<f", x))[0]
    bits = (bits + 0x7FFF + ((bits >> 16) & 1)) & 0xFFFF0000
    return struct.unpack("<f", struct.pack("<I", bits))[0]


def _bf16_split3(c):
    c1 = _bf16_round(c)
    c2 = _bf16_round(c - c1)
    c3 = _bf16_round(c - c1 - c2)
    return c1, c2, c3


def _attn_kernel(qt_ref, k_ref, vt_ref, za_ref, lq1_ref, lk1_ref, lq2_ref, lk2_ref, subg_ref,
                 o_ref, qaug_ref, e_ref, tile_ref, acc_ref, m_ref, alpha_ref, s_ref, smax_ref, p_ref):
    i = pl.program_id(1)
    t0 = i * TQ
    slopes2 = [s * LOG2E for s in SLOPES]

    @pl.when((pl.program_id(0) == 0) & (i == 0))
    def _():
        sl = lax.broadcasted_iota(jnp.int32, (TK, TQ), 0)
        tl = lax.broadcasted_iota(jnp.int32, (TK, TQ), 1)
        allowed = (sl // CHUNK) <= (tl // CHUNK)
        rel = (tl - jnp.abs(tl - sl) - sl).astype(F32)
        crow = lax.broadcasted_iota(jnp.int32, (2 * HEAD_DIM, 2 * TQ), 0)
        for h in range(N_HEADS):
            tile_ref[h] = jnp.where(allowed, slopes2[h] * rel, -jnp.inf)
            c1, c2, c3 = _bf16_split3(slopes2[h])
            consts = jnp.where(crow == 0, c1, jnp.where(crow == 1, c2, jnp.where(crow == 2, c3, 0.0)))
            qaug_ref[h, 2 * HEAD_DIM:, :] = consts.astype(BF16)
        lane = lax.broadcasted_iota(jnp.int32, (TK, 2 * HEAD_DIM), 1)
        row = lax.broadcasted_iota(jnp.int32, (TK, 2 * HEAD_DIM), 0)
        e_ref[...] = jnp.where(lane < 3, row, 0).astype(F32).astype(BF16)

    zeros = jnp.zeros((HEAD_DIM, TQ), BF16)
    for h in range(N_HEADS):
        qt = qt_ref[0, h]
        qaug_ref[h, 0:2 * HEAD_DIM, :] = jnp.concatenate(
            [jnp.concatenate([qt[:HEAD_DIM], zeros], axis=0),
             jnp.concatenate([zeros, qt[HEAD_DIM:]], axis=0)], axis=1)

    ones = jnp.ones((16, TK), BF16)

    def scores(h, j):
        j0 = pl.multiple_of(j * TK, TK)
        lhs = jnp.concatenate([k_ref[0, pl.ds(j0, TK), h * V_DIM:(h + 1) * V_DIM], e_ref[...]], axis=1)
        return jnp.dot(lhs, qaug_ref[h], preferred_element_type=F32)

    def colmax(s):
        return jnp.max(s, axis=0, keepdims=True)

    def softmax(h, j, first):
        if first:
            m_new = smax_ref[h]
            base = m_new
            alpha_ref[h] = jnp.zeros((1, 2 * TQ), F32)
        else:
            shift = slopes2[h] * jnp.full((1, 2 * TQ), j * TK - t0, jnp.int32).astype(F32)
            m_old = m_ref[h]
            m_new = jnp.maximum(m_old, smax_ref[h] + shift)
            base = m_new - shift
            alpha_ref[h] = jnp.exp2(m_old - m_new)
        p_ref[h] = jnp.exp2(s_ref[h] - base).astype(BF16)
        m_ref[h] = m_new

    def accumulate(h, j):
        lhs_v = jnp.concatenate([vt_ref[j, h], ones], axis=0)
        pv = jnp.dot(lhs_v, p_ref[h], preferred_element_type=F32)
        acc_ref[h] = acc_ref[h] * alpha_ref[h] + pv

    for h in range(N_HEADS):
        acc_ref[h] = jnp.zeros((V_DIM + 16, 2 * TQ), F32)
        s_cur = scores(h, i) + jnp.concatenate([tile_ref[h], tile_ref[h]], axis=1)
        s_ref[h] = s_cur
        smax_ref[h] = colmax(s_cur)
        s_nxt = scores(h, 0)
        m_nxt = colmax(s_nxt)
        softmax(h, i, True)
        s_ref[h] = s_nxt
        smax_ref[h] = m_nxt

    def body(t, carry):
        prev = jnp.where(t == 0, i, t - 1)
        nxt = jnp.minimum(t + 1, i)
        for h in range(N_HEADS):
            s_nxt = scores(h, nxt)
            m_nxt = colmax(s_nxt)
            accumulate(h, prev)
            softmax(h, t, False)
            s_ref[h] = s_nxt
            smax_ref[h] = m_nxt
        return carry

    lax.fori_loop(0, i, body, 0)
    lam = (jnp.exp(jnp.sum(lq1_ref[...] * lk1_ref[...], axis=-1, keepdims=True))
           - jnp.exp(jnp.sum(lq2_ref[...] * lk2_ref[...], axis=-1, keepdims=True))
           + LAMBDA_INIT)
    for h in range(N_HEADS):
        accumulate(h, jnp.where(i == 0, i, i - 1))
        hs = slice(h * V_DIM, (h + 1) * V_DIM)
        a = acc_ref[h]
        r = 1.0 / a[V_DIM:V_DIM + 1, :]
        ot = a[:V_DIM, :TQ] * r[:, :TQ] - lam * (a[:V_DIM, TQ:] * r[:, TQ:])
        ot = ot * lax.rsqrt(jnp.mean(ot * ot, axis=0, keepdims=True) + EPS)
        o = ot.T * (subg_ref[...] * (1.0 - LAMBDA_INIT))
        o_ref[0, :, hs] = (o * za_ref[0, :, hs].astype(F32)).astype(BF16)


def _attn(qt, k, vt, za, lq1, lk1, lq2, lk2, subg):
    B, S, _ = k.shape
    nq = S // TQ
    blk = lambda b, i: (b, i, 0)
    whole = lambda b, i: (b, 0, 0)
    fixed = lambda b, i: (0, 0)
    return pl.pallas_call(
        _attn_kernel,
        grid=(B, nq),
        in_specs=[
            pl.BlockSpec((1, N_HEADS, 2 * HEAD_DIM, TQ), lambda b, i: (b * nq + i, 0, 0, 0)),
            pl.BlockSpec((1, S, D_QK), whole),
            pl.BlockSpec((S // TK, N_HEADS, V_DIM, TK), lambda b, i: (b, 0, 0, 0)),
            pl.BlockSpec((1, TQ, D_ATTN), blk),
            pl.BlockSpec((1, HEAD_DIM), fixed),
            pl.BlockSpec((1, HEAD_DIM), fixed),
            pl.BlockSpec((1, HEAD_DIM), fixed),
            pl.BlockSpec((1, HEAD_DIM), fixed),
            pl.BlockSpec((1, V_DIM), fixed),
        ],
        out_specs=pl.BlockSpec((1, TQ, D_ATTN), blk),
        out_shape=jax.ShapeDtypeStruct((B, S, D_ATTN), BF16),
        scratch_shapes=[
            pltpu.VMEM((N_HEADS, 4 * HEAD_DIM, 2 * TQ), BF16),
            pltpu.VMEM((TK, 2 * HEAD_DIM), BF16),
            pltpu.VMEM((N_HEADS, TK, TQ), F32),
            pltpu.VMEM((N_HEADS, V_DIM + 16, 2 * TQ), F32),
            pltpu.VMEM((N_HEADS, 1, 2 * TQ), F32),
            pltpu.VMEM((N_HEADS, 1, 2 * TQ), F32),
            pltpu.VMEM((N_HEADS, TK, 2 * TQ), F32),
            pltpu.VMEM((N_HEADS, 1, 2 * TQ), F32),
            pltpu.VMEM((N_HEADS, TK, 2 * TQ), BF16),
        ],
        compiler_params=pltpu.CompilerParams(
            dimension_semantics=("arbitrary", "arbitrary"), vmem_limit_bytes=VMEM_LIMIT_BYTES),
        name="attn",
    )(qt, k, vt, za, lq1, lk1, lq2, lk2, subg)


def _out_kernel(x_ref, hc_ref, ha_ref, g_ref, wc_ref, wa_ref, wo_ref, gpost_ref, o_ref):
    yc = jnp.dot(hc_ref[...], wc_ref[...], preferred_element_type=F32)
    ya = jnp.dot(ha_ref[...], wa_ref[...], preferred_element_type=F32)
    mixed = (g_ref[:, :D_MODEL].astype(F32) * yc + g_ref[:, D_MODEL:].astype(F32) * ya).astype(BF16)
    out = jnp.dot(mixed, wo_ref[...], preferred_element_type=F32)
    ms = jnp.mean(out * out, axis=-1, keepdims=True)
    o_ref[...] = x_ref[...] + out * lax.rsqrt(ms + EPS) * gpost_ref[...]


def _out(x2, hc, ha, g, wc, wa, wo, gpost):
    T = x2.shape[0]
    row = lambda i: (i, 0)
    fixed = lambda i: (0, 0)
    return pl.pallas_call(
        _out_kernel,
        grid=(T // TM_OUT,),
        in_specs=[
            pl.BlockSpec((TM_OUT, D_MODEL), row),
            pl.BlockSpec((TM_OUT, D_CONV), row),
            pl.BlockSpec((TM_OUT, D_ATTN), row),
            pl.BlockSpec((TM_OUT, 2 * D_MODEL), row),
            pl.BlockSpec((D_CONV, D_MODEL), fixed),
            pl.BlockSpec((D_ATTN, D_MODEL), fixed),
            pl.BlockSpec((D_MODEL, D_MODEL), fixed),
            pl.BlockSpec((1, D_MODEL), fixed),
        ],
        out_specs=pl.BlockSpec((TM_OUT, D_MODEL), row),
        out_shape=jax.ShapeDtypeStruct((T, D_MODEL), F32),
        compiler_params=pltpu.CompilerParams(
            dimension_semantics=("arbitrary",), vmem_limit_bytes=VMEM_LIMIT_BYTES),
        name="out",
    )(x2, hc, ha, g, wc, wa, wo, gpost)


def kernel(x, w_in, conv_w, conv_b, conv_ln_g, conv_ln_b, w_conv_proj, lambda_q1, lambda_k1,
           lambda_q2, lambda_k2, subln_g, w_attn_proj, w_out, norm_pre_g, norm_post_g):
    B, S, D = x.shape
    assert (D, w_in.shape[0]) == (D_MODEL, 1) and TQ == TK
    assert S % max(TQ, TS_CONV) == 0 and (B * S) % TM_IN == 0 and TS_CONV % CONV_ROWS == 0
    T = B * S
    x2 = x.reshape(T, D)
    glu, zc, qt, k, vt, za, g = _inproj(x2, norm_pre_g, w_in[0].astype(BF16))
    seq = lambda a: a.reshape(B, S, a.shape[-1])
    hc = _conv(seq(glu), seq(zc), conv_w[0], conv_b, conv_ln_g, conv_ln_b)
    ha = _attn(qt, seq(k), vt, seq(za), lambda_q1, lambda_k1, lambda_q2, lambda_k2, subln_g)
    y = _out(x2, hc.reshape(T, D_CONV), ha.reshape(T, D_ATTN), g,
             w_conv_proj[0].astype(BF16), w_attn_proj[0].astype(BF16), w_out[0].astype(BF16),
             norm_post_g)
    return y.reshape(B, S, D)
```

```python
import math
import struct

import jax
import jax.numpy as jnp
from jax import lax
from jax.experimental import pallas as pl
from jax.experimental.pallas import tpu as pltpu

D_MODEL = 1024
CHUNK = 64
CONV_WIDTH = 31
D_CONV = 512
N_HEADS = 4
HEAD_DIM = 64
V_DIM = 128
D_ATTN = 512
D_QK = 512
EPS = 1e-6
LAMBDA_INIT = 0.8 - 0.6 * math.exp(-0.3 * 0)
LOG2E = math.log2(math.e)
SLOPES = tuple(2.0 ** (-8.0 * (h + 1) / N_HEADS) for h in range(N_HEADS))

BF16 = jnp.bfloat16
F32 = jnp.float32

VMEM_LIMIT_BYTES = 56 * 1024 * 1024

TM_IN = 1024
TS_CONV = 1024
CONV_HALO = 32
CONV_ROWS = 256
TQ = 256
TK = 256
TM_OUT = 1024


def _sigmoid(x):
    return 1.0 / (1.0 + jnp.exp(-x))


def _silu(x):
    return x * _sigmoid(x)


def _inproj_kernel(x_ref, gpre_ref, w_ref, glu_ref, zc_ref, qt_ref, k_ref, vt_ref, za_ref, g_ref):
    x = x_ref[...]
    ms = jnp.mean(x * x, axis=-1, keepdims=True)
    h = (x * lax.rsqrt(ms + EPS) * gpre_ref[...]).astype(BF16)

    def proj(c0, width):
        return jnp.dot(h, w_ref[:, c0:c0 + width], preferred_element_type=F32)

    def store_transposed(t_ref, y):
        for blk in range(TM_IN // TK):
            for hd in range(N_HEADS):
                t_ref[blk, hd] = y[blk * TK:(blk + 1) * TK, hd * V_DIM:(hd + 1) * V_DIM].T.astype(BF16)

    a = proj(0, D_CONV)
    b = proj(D_CONV, D_CONV)
    glu_ref[...] = (a * _sigmoid(b)).astype(BF16)
    zc_ref[...] = _silu(proj(2 * D_CONV, D_CONV)).astype(BF16)
    c0 = 3 * D_CONV
    store_transposed(qt_ref, proj(c0, D_QK) * (LOG2E / math.sqrt(HEAD_DIM)))
    k_ref[...] = proj(c0 + D_QK, D_QK).astype(BF16)
    store_transposed(vt_ref, proj(c0 + 2 * D_QK, D_ATTN))
    za_ref[...] = _silu(proj(c0 + 2 * D_QK + D_ATTN, D_ATTN)).astype(BF16)
    c0 = c0 + 2 * D_QK + 2 * D_ATTN
    for j in range(2 * D_MODEL // 512):
        g_ref[:, j * 512:(j + 1) * 512] = _sigmoid(proj(c0 + j * 512, 512)).astype(BF16)


def _inproj(x2, gpre, w_in):
    T = x2.shape[0]
    d_in = w_in.shape[1]
    row = lambda i: (i, 0)
    fixed = lambda i: (0, 0)
    tr_shape = (T // TK, N_HEADS, V_DIM, TK)
    tr_spec = pl.BlockSpec((TM_IN // TK, N_HEADS, V_DIM, TK), lambda i: (i, 0, 0, 0))
    rows = lambda w: (pl.BlockSpec((TM_IN, w), row), jax.ShapeDtypeStruct((T, w), BF16))
    tr = (tr_spec, jax.ShapeDtypeStruct(tr_shape, BF16))
    outs = (rows(D_CONV), rows(D_CONV), tr, rows(D_QK), tr, rows(D_ATTN), rows(2 * D_MODEL))
    return pl.pallas_call(
        _inproj_kernel,
        grid=(T // TM_IN,),
        in_specs=[
            pl.BlockSpec((TM_IN, D_MODEL), row),
            pl.BlockSpec((1, D_MODEL), fixed),
            pl.BlockSpec((D_MODEL, d_in), fixed, pipeline_mode=pl.Buffered(1)),
        ],
        out_specs=[o[0] for o in outs],
        out_shape=[o[1] for o in outs],
        compiler_params=pltpu.CompilerParams(
            dimension_semantics=("arbitrary",), vmem_limit_bytes=VMEM_LIMIT_BYTES),
        name="inproj",
    )(x2, gpre, w_in)


def _conv_kernel(glu_ref, halo_ref, zc_ref, w_ref, b_ref, lng_ref, lnb_ref, o_ref, xpad_ref, acc_ref):
    i = pl.program_id(1)
    halo = jnp.where(i == 0, 0.0, halo_ref[0].astype(F32))
    cur = glu_ref[0].astype(F32)
    for c in range(D_CONV // 128):
        xpad_ref[c, 0:CONV_HALO, :] = halo[:, c * 128:(c + 1) * 128]
        xpad_ref[c, CONV_HALO:, :] = cur[:, c * 128:(c + 1) * 128]

    shift = CONV_HALO - (CONV_WIDTH - 1)

    def rows(r, carry):
        r0 = pl.multiple_of(r * CONV_ROWS, CONV_ROWS)
        for c in range(D_CONV // 128):
            acc = jnp.zeros((CONV_ROWS, 128), F32)
            for j in range(CONV_WIDTH):
                xs = xpad_ref[c, pl.ds(r0 + (shift + j), CONV_ROWS), :]
                acc = acc + xs * w_ref[j:j + 1, c * 128:(c + 1) * 128]
            acc_ref[pl.ds(r0, CONV_ROWS), c * 128:(c + 1) * 128] = acc
        return carry

    lax.fori_loop(0, TS_CONV // CONV_ROWS, rows, 0)

    y = acc_ref[...] + b_ref[...]
    mu = jnp.mean(y, axis=-1, keepdims=True)
    yc = y - mu
    var = jnp.mean(yc * yc, axis=-1, keepdims=True)
    yn = yc * lax.rsqrt(var + EPS) * lng_ref[...] + lnb_ref[...]
    o_ref[0] = (_silu(yn) * zc_ref[0].astype(F32)).astype(BF16)


def _conv(glu, zc, conv_w, conv_b, ln_g, ln_b):
    B, S, _ = glu.shape
    blocks_per_halo = TS_CONV // CONV_HALO
    cur = lambda b, i: (b, i, 0)
    prev = lambda b, i: (b, jnp.maximum(i * blocks_per_halo - 1, 0), 0)
    fixed = lambda b, i: (0, 0)
    return pl.pallas_call(
        _conv_kernel,
        grid=(B, S // TS_CONV),
        in_specs=[
            pl.BlockSpec((1, TS_CONV, D_CONV), cur),
            pl.BlockSpec((1, CONV_HALO, D_CONV), prev),
            pl.BlockSpec((1, TS_CONV, D_CONV), cur),
            pl.BlockSpec((CONV_WIDTH, D_CONV), fixed),
            pl.BlockSpec((1, D_CONV), fixed),
            pl.BlockSpec((1, D_CONV), fixed),
            pl.BlockSpec((1, D_CONV), fixed),
        ],
        out_specs=pl.BlockSpec((1, TS_CONV, D_CONV), cur),
        out_shape=jax.ShapeDtypeStruct((B, S, D_CONV), BF16),
        scratch_shapes=[
            pltpu.VMEM((D_CONV // 128, TS_CONV + CONV_HALO, 128), F32),
            pltpu.VMEM((TS_CONV, D_CONV), F32),
        ],
        compiler_params=pltpu.CompilerParams(
            dimension_semantics=("arbitrary", "arbitrary"), vmem_limit_bytes=VMEM_LIMIT_BYTES),
        name="conv",
    )(glu, glu, zc, conv_w, conv_b, ln_g, ln_b)


def _bf16_round(x):
    bits = struct.unpack("<I", struct.pack("<f", x))[0]
    bits = (bits + 0x7FFF + ((bits >> 16) & 1)) & 0xFFFF0000
    return struct.unpack("<f", struct.pack("<I", bits))[0]


def _bf16_split3(c):
    c1 = _bf16_round(c)
    c2 = _bf16_round(c - c1)
    c3 = _bf16_round(c - c1 - c2)
    return c1, c2, c3


def _attn_kernel(qt_ref, k_ref, vt_ref, za_ref, lq1_ref, lk1_ref, lq2_ref, lk2_ref, subg_ref,
                 o_ref, qaug_ref, e_ref, tile_ref, acc_ref, m_ref, s_ref, smax_ref, p_ref):
    i = pl.program_id(1)
    t0 = i * TQ
    slopes2 = [s * LOG2E for s in SLOPES]

    @pl.when((pl.program_id(0) == 0) & (i == 0))
    def _():
        sl = lax.broadcasted_iota(jnp.int32, (TK, 2 * TQ), 0)
        tl = lax.broadcasted_iota(jnp.int32, (TK, 2 * TQ), 1) % TQ
        allowed = (sl // CHUNK) <= (tl // CHUNK)
        rel = (tl - jnp.abs(tl - sl) - sl).astype(F32)
        crow = lax.broadcasted_iota(jnp.int32, (2 * HEAD_DIM, 2 * TQ), 0)
        for h in range(N_HEADS):
            tile_ref[h] = jnp.where(allowed, slopes2[h] * rel, -jnp.inf)
            c1, c2, c3 = _bf16_split3(slopes2[h])
            consts = jnp.where(crow == 0, c1, jnp.where(crow == 1, c2, jnp.where(crow == 2, c3, 0.0)))
            qaug_ref[h, 2 * HEAD_DIM:, :] = consts.astype(BF16)
        lane = lax.broadcasted_iota(jnp.int32, (TK, 2 * HEAD_DIM), 1)
        row = lax.broadcasted_iota(jnp.int32, (TK, 2 * HEAD_DIM), 0)
        e_ref[...] = jnp.where(lane < 3, row, 0).astype(F32).astype(BF16)

    zeros = jnp.zeros((HEAD_DIM, TQ), BF16)
    for h in range(N_HEADS):
        qt = qt_ref[0, h]
        qaug_ref[h, 0:2 * HEAD_DIM, :] = jnp.concatenate(
            [jnp.concatenate([qt[:HEAD_DIM], zeros], axis=0),
             jnp.concatenate([zeros, qt[HEAD_DIM:]], axis=0)], axis=1)

    ones = jnp.ones((16, TK), BF16)

    def scores(h, j):
        j0 = pl.multiple_of(j * TK, TK)
        lhs = jnp.concatenate([k_ref[0, pl.ds(j0, TK), h * V_DIM:(h + 1) * V_DIM], e_ref[...]], axis=1)
        return jnp.dot(lhs, qaug_ref[h], preferred_element_type=F32)

    def colmax(s):
        return jnp.max(s, axis=0, keepdims=True)

    def update(h, j):
        shift = slopes2[h] * jnp.full((1, 2 * TQ), j * TK - t0, jnp.int32).astype(F32)
        m_old = m_ref[h]
        m_new = jnp.maximum(m_old, smax_ref[h] + shift)
        p = jnp.exp2(s_ref[h] - (m_new - shift)).astype(BF16)
        m_ref[h] = m_new
        lhs_v = jnp.concatenate([vt_ref[j, h], ones], axis=0)
        pv = jnp.dot(lhs_v, p, preferred_element_type=F32)
        acc_ref[h] = acc_ref[h] * jnp.exp2(m_old - m_new) + pv

    for h in range(N_HEADS):
        s_cur = scores(h, i) + tile_ref[h]
        m_cur = colmax(s_cur)
        s_nxt = scores(h, 0)
        m_nxt = colmax(s_nxt)
        p_ref[h] = jnp.exp2(s_cur - m_cur).astype(BF16)
        m_ref[h] = m_cur
        s_ref[h] = s_nxt
        smax_ref[h] = m_nxt
    for h in range(N_HEADS):
        lhs_v = jnp.concatenate([vt_ref[i, h], ones], axis=0)
        acc_ref[h] = jnp.dot(lhs_v, p_ref[h], preferred_element_type=F32)

    def body(t, carry):
        nxt = jnp.minimum(t + 1, i)
        for h in range(N_HEADS):
            s_nxt = scores(h, nxt)
            m_nxt = colmax(s_nxt)
            update(h, t)
            s_ref[h] = s_nxt
            smax_ref[h] = m_nxt
        return carry

    lax.fori_loop(0, i, body, 0)
    lam = (jnp.exp(jnp.sum(lq1_ref[...] * lk1_ref[...], axis=-1, keepdims=True))
           - jnp.exp(jnp.sum(lq2_ref[...] * lk2_ref[...], axis=-1, keepdims=True))
           + LAMBDA_INIT)
    for h in range(N_HEADS):
        hs = slice(h * V_DIM, (h + 1) * V_DIM)
        a = acc_ref[h]
        r = 1.0 / a[V_DIM:V_DIM + 1, :]
        ot = a[:V_DIM, :TQ] * r[:, :TQ] - lam * (a[:V_DIM, TQ:] * r[:, TQ:])
        ot = ot * lax.rsqrt(jnp.mean(ot * ot, axis=0, keepdims=True) + EPS)
        o = ot.T * (subg_ref[...] * (1.0 - LAMBDA_INIT))
        o_ref[0, :, hs] = (o * za_ref[0, :, hs].astype(F32)).astype(BF16)


def _attn(qt, k, vt, za, lq1, lk1, lq2, lk2, subg):
    B, S, _ = k.shape
    nq = S // TQ
    blk = lambda b, i: (b, i, 0)
    whole = lambda b, i: (b, 0, 0)
    fixed = lambda b, i: (0, 0)
    return pl.pallas_call(
        _attn_kernel,
        grid=(B, nq),
        in_specs=[
            pl.BlockSpec((1, N_HEADS, 2 * HEAD_DIM, TQ), lambda b, i: (b * nq + i, 0, 0, 0)),
            pl.BlockSpec((1, S, D_QK), whole),
            pl.BlockSpec((S // TK, N_HEADS, V_DIM, TK), lambda b, i: (b, 0, 0, 0)),
            pl.BlockSpec((1, TQ, D_ATTN), blk),
            pl.BlockSpec((1, HEAD_DIM), fixed),
            pl.BlockSpec((1, HEAD_DIM), fixed),
            pl.BlockSpec((1, HEAD_DIM), fixed),
            pl.BlockSpec((1, HEAD_DIM), fixed),
            pl.BlockSpec((1, V_DIM), fixed),
        ],
        out_specs=pl.BlockSpec((1, TQ, D_ATTN), blk),
        out_shape=jax.ShapeDtypeStruct((B, S, D_ATTN), BF16),
        scratch_shapes=[
            pltpu.VMEM((N_HEADS, 4 * HEAD_DIM, 2 * TQ), BF16),
            pltpu.VMEM((TK, 2 * HEAD_DIM), BF16),
            pltpu.VMEM((N_HEADS, TK, 2 * TQ), F32),
            pltpu.VMEM((N_HEADS, V_DIM + 16, 2 * TQ), F32),
            pltpu.VMEM((N_HEADS, 1, 2 * TQ), F32),
            pltpu.VMEM((N_HEADS, TK, 2 * TQ), F32),
            pltpu.VMEM((N_HEADS, 1, 2 * TQ), F32),
            pltpu.VMEM((N_HEADS, TK, 2 * TQ), BF16),
        ],
        compiler_params=pltpu.CompilerParams(
            dimension_semantics=("arbitrary", "arbitrary"), vmem_limit_bytes=VMEM_LIMIT_BYTES),
        name="attn",
    )(qt, k, vt, za, lq1, lk1, lq2, lk2, subg)


def _out_kernel(x_ref, hc_ref, ha_ref, g_ref, wc_ref, wa_ref, wo_ref, gpost_ref, o_ref):
    yc = jnp.dot(hc_ref[...], wc_ref[...], preferred_element_type=F32)
    ya = jnp.dot(ha_ref[...], wa_ref[...], preferred_element_type=F32)
    mixed = (g_ref[:, :D_MODEL].astype(F32) * yc + g_ref[:, D_MODEL:].astype(F32) * ya).astype(BF16)
    out = jnp.dot(mixed, wo_ref[...], preferred_element_type=F32)
    ms = jnp.mean(out * out, axis=-1, keepdims=True)
    o_ref[...] = x_ref[...] + out * lax.rsqrt(ms + EPS) * gpost_ref[...]


def _out(x2, hc, ha, g, wc, wa, wo, gpost):
    T = x2.shape[0]
    row = lambda i: (i, 0)
    fixed = lambda i: (0, 0)
    return pl.pallas_call(
        _out_kernel,
        grid=(T // TM_OUT,),
        in_specs=[
            pl.BlockSpec((TM_OUT, D_MODEL), row),
            pl.BlockSpec((TM_OUT, D_CONV), row),
            pl.BlockSpec((TM_OUT, D_ATTN), row),
            pl.BlockSpec((TM_OUT, 2 * D_MODEL), row),
            pl.BlockSpec((D_CONV, D_MODEL), fixed),
            pl.BlockSpec((D_ATTN, D_MODEL), fixed),
            pl.BlockSpec((D_MODEL, D_MODEL), fixed),
            pl.BlockSpec((1, D_MODEL), fixed),
        ],
        out_specs=pl.BlockSpec((TM_OUT, D_MODEL), row),
        out_shape=jax.ShapeDtypeStruct((T, D_MODEL), F32),
        compiler_params=pltpu.CompilerParams(
            dimension_semantics=("arbitrary",), vmem_limit_bytes=VMEM_LIMIT_BYTES),
        name="out",
    )(x2, hc, ha, g, wc, wa, wo, gpost)


def kernel(x, w_in, conv_w, conv_b, conv_ln_g, conv_ln_b, w_conv_proj, lambda_q1, lambda_k1,
           lambda_q2, lambda_k2, subln_g, w_attn_proj, w_out, norm_pre_g, norm_post_g):
    B, S, D = x.shape
    assert (D, w_in.shape[0]) == (D_MODEL, 1) and TQ == TK
    assert S % max(TQ, TS_CONV) == 0 and (B * S) % TM_IN == 0
    T = B * S
    x2 = x.reshape(T, D)
    glu, zc, qt, k, vt, za, g = _inproj(x2, norm_pre_g, w_in[0].astype(BF16))
    seq = lambda a: a.reshape(B, S, a.shape[-1])
    hc = _conv(seq(glu), seq(zc), conv_w[0], conv_b, conv_ln_g, conv_ln_b)
    ha = _attn(qt, seq(k), vt, seq(za), lambda_q1, lambda_k1, lambda_q2, lambda_k2, subln_g)
    y = _out(x2, hc.reshape(T, D_CONV), ha.reshape(T, D_ATTN), g,
             w_conv_proj[0].astype(BF16), w_attn_proj[0].astype(BF16), w_out[0].astype(BF16),
             norm_post_g)
    return y.reshape(B, S, D)
```

```python
import math
import struct

import jax
import jax.numpy as jnp
from jax import lax
from jax.experimental import pallas as pl
from jax.experimental.pallas import tpu as pltpu

D_MODEL = 1024
CHUNK = 64
CONV_WIDTH = 31
D_CONV = 512
N_HEADS = 4
HEAD_DIM = 64
V_DIM = 128
D_ATTN = 512
D_QK = 512
EPS = 1e-6
LAMBDA_INIT = 0.8 - 0.6 * math.exp(-0.3 * 0)
LOG2E = math.log2(math.e)
SLOPES = tuple(2.0 ** (-8.0 * (h + 1) / N_HEADS) for h in range(N_HEADS))

BF16 = jnp.bfloat16
F32 = jnp.float32

VMEM_LIMIT_BYTES = 56 * 1024 * 1024

TM_IN = 1024
TS_CONV = 1024
CONV_HALO = 32
CONV_ROWS = 256
TQ = 256
TK = 256
TM_OUT = 1024


def _sigmoid(x):
    return 1.0 / (1.0 + jnp.exp(-x))


def _silu(x):
    return x * _sigmoid(x)


def _inproj_kernel(x_ref, gpre_ref, w_ref, glu_ref, zc_ref, qt_ref, k_ref, vt_ref, za_ref, g_ref):
    x = x_ref[...]
    ms = jnp.mean(x * x, axis=-1, keepdims=True)
    h = (x * lax.rsqrt(ms + EPS) * gpre_ref[...]).astype(BF16)

    def proj(c0, width):
        return jnp.dot(h, w_ref[:, c0:c0 + width], preferred_element_type=F32)

    def store_transposed(t_ref, y):
        for blk in range(TM_IN // TK):
            for hd in range(N_HEADS):
                t_ref[blk, hd] = y[blk * TK:(blk + 1) * TK, hd * V_DIM:(hd + 1) * V_DIM].T.astype(BF16)

    a = proj(0, D_CONV)
    b = proj(D_CONV, D_CONV)
    glu_ref[...] = (a * _sigmoid(b)).astype(BF16)
    zc_ref[...] = _silu(proj(2 * D_CONV, D_CONV)).astype(BF16)
    c0 = 3 * D_CONV
    store_transposed(qt_ref, proj(c0, D_QK) * (LOG2E / math.sqrt(HEAD_DIM)))
    k_ref[...] = proj(c0 + D_QK, D_QK).astype(BF16)
    store_transposed(vt_ref, proj(c0 + 2 * D_QK, D_ATTN))
    za_ref[...] = _silu(proj(c0 + 2 * D_QK + D_ATTN, D_ATTN)).astype(BF16)
    c0 = c0 + 2 * D_QK + 2 * D_ATTN
    for j in range(2 * D_MODEL // 512):
        g_ref[:, j * 512:(j + 1) * 512] = _sigmoid(proj(c0 + j * 512, 512)).astype(BF16)


def _inproj(x2, gpre, w_in):
    T = x2.shape[0]
    d_in = w_in.shape[1]
    row = lambda i: (i, 0)
    fixed = lambda i: (0, 0)
    tr_shape = (T // TK, N_HEADS, V_DIM, TK)
    tr_spec = pl.BlockSpec((TM_IN // TK, N_HEADS, V_DIM, TK), lambda i: (i, 0, 0, 0))
    rows = lambda w: (pl.BlockSpec((TM_IN, w), row), jax.ShapeDtypeStruct((T, w), BF16))
    tr = (tr_spec, jax.ShapeDtypeStruct(tr_shape, BF16))
    outs = (rows(D_CONV), rows(D_CONV), tr, rows(D_QK), tr, rows(D_ATTN), rows(2 * D_MODEL))
    return pl.pallas_call(
        _inproj_kernel,
        grid=(T // TM_IN,),
        in_specs=[
            pl.BlockSpec((TM_IN, D_MODEL), row),
            pl.BlockSpec((1, D_MODEL), fixed),
            pl.BlockSpec((D_MODEL, d_in), fixed, pipeline_mode=pl.Buffered(1)),
        ],
        out_specs=[o[0] for o in outs],
        out_shape=[o[1] for o in outs],
        compiler_params=pltpu.CompilerParams(
            dimension_semantics=("arbitrary",), vmem_limit_bytes=VMEM_LIMIT_BYTES),
        name="inproj",
    )(x2, gpre, w_in)


def _conv_kernel(glu_ref, halo_ref, zc_ref, w_ref, b_ref, lng_ref, lnb_ref, o_ref, xpad_ref, acc_ref):
    i = pl.program_id(1)
    halo = jnp.where(i == 0, 0.0, halo_ref[0].astype(F32))
    cur = glu_ref[0].astype(F32)
    for c in range(D_CONV // 128):
        xpad_ref[c, 0:CONV_HALO, :] = halo[:, c * 128:(c + 1) * 128]
        xpad_ref[c, CONV_HALO:, :] = cur[:, c * 128:(c + 1) * 128]

    shift = CONV_HALO - (CONV_WIDTH - 1)

    def rows(r, carry):
        r0 = pl.multiple_of(r * CONV_ROWS, CONV_ROWS)
        for c in range(D_CONV // 128):
            acc = jnp.zeros((CONV_ROWS, 128), F32)
            for j in range(CONV_WIDTH):
                xs = xpad_ref[c, pl.ds(r0 + (shift + j), CONV_ROWS), :]
                acc = acc + xs * w_ref[j:j + 1, c * 128:(c + 1) * 128]
            acc_ref[pl.ds(r0, CONV_ROWS), c * 128:(c + 1) * 128] = acc
        return carry

    lax.fori_loop(0, TS_CONV // CONV_ROWS, rows, 0)

    y = acc_ref[...] + b_ref[...]
    mu = jnp.mean(y, axis=-1, keepdims=True)
    yc = y - mu
    var = jnp.mean(yc * yc, axis=-1, keepdims=True)
    yn = yc * lax.rsqrt(var + EPS) * lng_ref[...] + lnb_ref[...]
    o_ref[0] = (_silu(yn) * zc_ref[0].astype(F32)).astype(BF16)


def _conv(glu, zc, conv_w, conv_b, ln_g, ln_b):
    B, S, _ = glu.shape
    blocks_per_halo = TS_CONV // CONV_HALO
    cur = lambda b, i: (b, i, 0)
    prev = lambda b, i: (b, jnp.maximum(i * blocks_per_halo - 1, 0), 0)
    fixed = lambda b, i: (0, 0)
    return pl.pallas_call(
        _conv_kernel,
        grid=(B, S // TS_CONV),
        in_specs=[
            pl.BlockSpec((1, TS_CONV, D_CONV), cur),
            pl.BlockSpec((1, CONV_HALO, D_CONV), prev),
            pl.BlockSpec((1, TS_CONV, D_CONV), cur),
            pl.BlockSpec((CONV_WIDTH, D_CONV), fixed),
            pl.BlockSpec((1, D_CONV), fixed),
            pl.BlockSpec((1, D_CONV), fixed),
            pl.BlockSpec((1, D_CONV), fixed),
        ],
        out_specs=pl.BlockSpec((1, TS_CONV, D_CONV), cur),
        out_shape=jax.ShapeDtypeStruct((B, S, D_CONV), BF16),
        scratch_shapes=[
            pltpu.VMEM((D_CONV // 128, TS_CONV + CONV_HALO, 128), F32),
            pltpu.VMEM((TS_CONV, D_CONV), F32),
        ],
        compiler_params=pltpu.CompilerParams(
            dimension_semantics=("arbitrary", "arbitrary"), vmem_limit_bytes=VMEM_LIMIT_BYTES),
        name="conv",
    )(glu, glu, zc, conv_w, conv_b, ln_g, ln_b)


def _bf16_round(x):
    bits = struct.unpack("<I", struct.pack("<f", x))[0]
    bits = (bits + 0x7FFF + ((bits >> 16) & 1)) & 0xFFFF0000
    return struct.unpack("<f", struct.pack("<I", bits))[0]


def _bf16_split3(c):
    c1 = _bf16_round(c)
    c2 = _bf16_round(c - c1)
    c3 = _bf16_round(c - c1 - c2)
    return c1, c2, c3


def _attn_kernel(qt_ref, k_ref, vt_ref, za_ref, lq1_ref, lk1_ref, lq2_ref, lk2_ref, subg_ref,
                 o_ref, qaug_ref, e_ref, tile_ref, acc_ref, m_ref, s_ref, smax_ref, p_ref):
    i = pl.program_id(1)
    t0 = i * TQ
    slopes2 = [s * LOG2E for s in SLOPES]

    @pl.when((pl.program_id(0) == 0) & (i == 0))
    def _():
        sl = lax.broadcasted_iota(jnp.int32, (TK, 2 * TQ), 0)
        tl = lax.broadcasted_iota(jnp.int32, (TK, 2 * TQ), 1) % TQ
        allowed = (sl // CHUNK) <= (tl // CHUNK)
        rel = (tl - jnp.abs(tl - sl) - sl).astype(F32)
        crow = lax.broadcasted_iota(jnp.int32, (2 * HEAD_DIM, 2 * TQ), 0)
        for h in range(N_HEADS):
            tile_ref[h] = jnp.where(allowed, slopes2[h] * rel, -jnp.inf)
            c1, c2, c3 = _bf16_split3(slopes2[h])
            consts = jnp.where(crow == 0, c1, jnp.where(crow == 1, c2, jnp.where(crow == 2, c3, 0.0)))
            qaug_ref[h, 2 * HEAD_DIM:, :] = consts.astype(BF16)
        lane = lax.broadcasted_iota(jnp.int32, (TK, 2 * HEAD_DIM), 1)
        row = lax.broadcasted_iota(jnp.int32, (TK, 2 * HEAD_DIM), 0)
        e_ref[...] = jnp.where(lane < 3, row, 0).astype(F32).astype(BF16)

    zeros = jnp.zeros((HEAD_DIM, TQ), BF16)
    for h in range(N_HEADS):
        qt = qt_ref[0, h]
        qaug_ref[h, 0:2 * HEAD_DIM, :] = jnp.concatenate(
            [jnp.concatenate([qt[:HEAD_DIM], zeros], axis=0),
             jnp.concatenate([zeros, qt[HEAD_DIM:]], axis=0)], axis=1)

    ones = jnp.ones((16, TK), BF16)
    ones2 = jnp.ones((16, 2 * TK), BF16)

    def scores(h, j):
        j0 = pl.multiple_of(j * TK, TK)
        lhs = jnp.concatenate([k_ref[0, pl.ds(j0, TK), h * V_DIM:(h + 1) * V_DIM], e_ref[...]], axis=1)
        return jnp.dot(lhs, qaug_ref[h], preferred_element_type=F32)

    def colmax(s):
        return jnp.max(s, axis=0, keepdims=True)

    def shift_of(h, j):
        return slopes2[h] * jnp.full((1, 2 * TQ), j * TK - t0, jnp.int32).astype(F32)

    def update_pair(h, ja):
        sh_a, sh_b = shift_of(h, ja), shift_of(h, ja + 1)
        m_old = m_ref[h]
        m_new = jnp.maximum(m_old, jnp.maximum(smax_ref[0, h] + sh_a, smax_ref[1, h] + sh_b))
        p = jnp.concatenate([jnp.exp2(s_ref[0, h] - (m_new - sh_a)).astype(BF16),
                             jnp.exp2(s_ref[1, h] - (m_new - sh_b)).astype(BF16)], axis=0)
        m_ref[h] = m_new
        lhs_v = jnp.concatenate([jnp.concatenate([vt_ref[ja, h], vt_ref[ja + 1, h]], axis=1), ones2], axis=0)
        pv = jnp.dot(lhs_v, p, preferred_element_type=F32)
        acc_ref[h] = acc_ref[h] * jnp.exp2(m_old - m_new) + pv

    def update_single(h, j):
        sh = shift_of(h, j)
        m_old = m_ref[h]
        m_new = jnp.maximum(m_old, smax_ref[0, h] + sh)
        p = jnp.exp2(s_ref[0, h] - (m_new - sh)).astype(BF16)
        m_ref[h] = m_new
        lhs_v = jnp.concatenate([vt_ref[j, h], ones], axis=0)
        pv = jnp.dot(lhs_v, p, preferred_element_type=F32)
        acc_ref[h] = acc_ref[h] * jnp.exp2(m_old - m_new) + pv

    n_blocks = k_ref.shape[1] // TK
    n_pairs = i // 2
    for h in range(N_HEADS):
        s_cur = scores(h, i) + tile_ref[h]
        m_cur = colmax(s_cur)
        s_a = scores(h, 0)
        s_b = scores(h, 1)
        p_ref[h] = jnp.exp2(s_cur - m_cur).astype(BF16)
        m_ref[h] = m_cur
        s_ref[0, h] = s_a
        smax_ref[0, h] = colmax(s_a)
        s_ref[1, h] = s_b
        smax_ref[1, h] = colmax(s_b)
    for h in range(N_HEADS):
        lhs_v = jnp.concatenate([vt_ref[i, h], ones], axis=0)
        acc_ref[h] = jnp.dot(lhs_v, p_ref[h], preferred_element_type=F32)

    def pair_body(jp, carry):
        ja = 2 * jp
        na = jnp.minimum(ja + 2, n_blocks - 1)
        nb = jnp.minimum(ja + 3, n_blocks - 1)
        for h in range(N_HEADS):
            s_a = scores(h, na)
            m_a = colmax(s_a)
            s_b = scores(h, nb)
            m_b = colmax(s_b)
            update_pair(h, ja)
            s_ref[0, h] = s_a
            smax_ref[0, h] = m_a
            s_ref[1, h] = s_b
            smax_ref[1, h] = m_b
        return carry

    lax.fori_loop(0, n_pairs, pair_body, 0)

    def single_body(_, carry):
        for h in range(N_HEADS):
            update_single(h, i - 1)
        return carry

    lax.fori_loop(0, i % 2, single_body, 0)
    lam = (jnp.exp(jnp.sum(lq1_ref[...] * lk1_ref[...], axis=-1, keepdims=True))
           - jnp.exp(jnp.sum(lq2_ref[...] * lk2_ref[...], axis=-1, keepdims=True))
           + LAMBDA_INIT)
    for h in range(N_HEADS):
        hs = slice(h * V_DIM, (h + 1) * V_DIM)
        a = acc_ref[h]
        r = 1.0 / a[V_DIM:V_DIM + 1, :]
        ot = a[:V_DIM, :TQ] * r[:, :TQ] - lam * (a[:V_DIM, TQ:] * r[:, TQ:])
        ot = ot * lax.rsqrt(jnp.mean(ot * ot, axis=0, keepdims=True) + EPS)
        o = ot.T * (subg_ref[...] * (1.0 - LAMBDA_INIT))
        o_ref[0, :, hs] = (o * za_ref[0, :, hs].astype(F32)).astype(BF16)


def _attn(qt, k, vt, za, lq1, lk1, lq2, lk2, subg):
    B, S, _ = k.shape
    nq = S // TQ
    blk = lambda b, i: (b, i, 0)
    whole = lambda b, i: (b, 0, 0)
    fixed = lambda b, i: (0, 0)
    return pl.pallas_call(
        _attn_kernel,
        grid=(B, nq),
        in_specs=[
            pl.BlockSpec((1, N_HEADS, 2 * HEAD_DIM, TQ), lambda b, i: (b * nq + i, 0, 0, 0)),
            pl.BlockSpec((1, S, D_QK), whole),
            pl.BlockSpec((S // TK, N_HEADS, V_DIM, TK), lambda b, i: (b, 0, 0, 0)),
            pl.BlockSpec((1, TQ, D_ATTN), blk),
            pl.BlockSpec((1, HEAD_DIM), fixed),
            pl.BlockSpec((1, HEAD_DIM), fixed),
            pl.BlockSpec((1, HEAD_DIM), fixed),
            pl.BlockSpec((1, HEAD_DIM), fixed),
            pl.BlockSpec((1, V_DIM), fixed),
        ],
        out_specs=pl.BlockSpec((1, TQ, D_ATTN), blk),
        out_shape=jax.ShapeDtypeStruct((B, S, D_ATTN), BF16),
        scratch_shapes=[
            pltpu.VMEM((N_HEADS, 4 * HEAD_DIM, 2 * TQ), BF16),
            pltpu.VMEM((TK, 2 * HEAD_DIM), BF16),
            pltpu.VMEM((N_HEADS, TK, 2 * TQ), F32),
            pltpu.VMEM((N_HEADS, V_DIM + 16, 2 * TQ), F32),
            pltpu.VMEM((N_HEADS, 1, 2 * TQ), F32),
            pltpu.VMEM((2, N_HEADS, TK, 2 * TQ), F32),
            pltpu.VMEM((2, N_HEADS, 1, 2 * TQ), F32),
            pltpu.VMEM((N_HEADS, TK, 2 * TQ), BF16),
        ],
        compiler_params=pltpu.CompilerParams(
            dimension_semantics=("arbitrary", "arbitrary"), vmem_limit_bytes=VMEM_LIMIT_BYTES),
        name="attn",
    )(qt, k, vt, za, lq1, lk1, lq2, lk2, subg)


def _out_kernel(x_ref, hc_ref, ha_ref, g_ref, wc_ref, wa_ref, wo_ref, gpost_ref, o_ref):
    yc = jnp.dot(hc_ref[...], wc_ref[...], preferred_element_type=F32)
    ya = jnp.dot(ha_ref[...], wa_ref[...], preferred_element_type=F32)
    mixed = (g_ref[:, :D_MODEL].astype(F32) * yc + g_ref[:, D_MODEL:].astype(F32) * ya).astype(BF16)
    out = jnp.dot(mixed, wo_ref[...], preferred_element_type=F32)
    ms = jnp.mean(out * out, axis=-1, keepdims=True)
    o_ref[...] = x_ref[...] + out * lax.rsqrt(ms + EPS) * gpost_ref[...]


def _out(x2, hc, ha, g, wc, wa, wo, gpost):
    T = x2.shape[0]
    row = lambda i: (i, 0)
    fixed = lambda i: (0, 0)
    return pl.pallas_call(
        _out_kernel,
        grid=(T // TM_OUT,),
        in_specs=[
            pl.BlockSpec((TM_OUT, D_MODEL), row),
            pl.BlockSpec((TM_OUT, D_CONV), row),
            pl.BlockSpec((TM_OUT, D_ATTN), row),
            pl.BlockSpec((TM_OUT, 2 * D_MODEL), row),
            pl.BlockSpec((D_CONV, D_MODEL), fixed),
            pl.BlockSpec((D_ATTN, D_MODEL), fixed),
            pl.BlockSpec((D_MODEL, D_MODEL), fixed),
            pl.BlockSpec((1, D_MODEL), fixed),
        ],
        out_specs=pl.BlockSpec((TM_OUT, D_MODEL), row),
        out_shape=jax.ShapeDtypeStruct((T, D_MODEL), F32),
        compiler_params=pltpu.CompilerParams(
            dimension_semantics=("arbitrary",), vmem_limit_bytes=VMEM_LIMIT_BYTES),
        name="out",
    )(x2, hc, ha, g, wc, wa, wo, gpost)


def kernel(x, w_in, conv_w, conv_b, conv_ln_g, conv_ln_b, w_conv_proj, lambda_q1, lambda_k1,
           lambda_q2, lambda_k2, subln_g, w_attn_proj, w_out, norm_pre_g, norm_post_g):
    B, S, D = x.shape
    assert (D, w_in.shape[0]) == (D_MODEL, 1) and TQ == TK
    assert S % max(TQ, TS_CONV) == 0 and (B * S) % TM_IN == 0
    T = B * S
    x2 = x.reshape(T, D)
    glu, zc, qt, k, vt, za, g = _inproj(x2, norm_pre_g, w_in[0].astype(BF16))
    seq = lambda a: a.reshape(B, S, a.shape[-1])
    hc = _conv(seq(glu), seq(zc), conv_w[0], conv_b, conv_ln_g, conv_ln_b)
    ha = _attn(qt, seq(k), vt, seq(za), lambda_q1, lambda_k1, lambda_q2, lambda_k2, subln_g)
    y = _out(x2, hc.reshape(T, D_CONV), ha.reshape(T, D_ATTN), g,
             w_conv_proj[0].astype(BF16), w_attn_proj[0].astype(BF16), w_out[0].astype(BF16),
             norm_post_g)
    return y.reshape(B, S, D)
```

```python
import math
import struct

import jax
import jax.numpy as jnp
from jax import lax
from jax.experimental import pallas as pl
from jax.experimental.pallas import tpu as pltpu

D_MODEL = 1024
CHUNK = 64
CONV_WIDTH = 31
D_CONV = 512
N_HEADS = 4
HEAD_DIM = 64
V_DIM = 128
D_ATTN = 512
D_QK = 512
EPS = 1e-6
LAMBDA_INIT = 0.8 - 0.6 * math.exp(-0.3 * 0)
LOG2E = math.log2(math.e)
SLOPES = tuple(2.0 ** (-8.0 * (h + 1) / N_HEADS) for h in range(N_HEADS))

BF16 = jnp.bfloat16
F32 = jnp.float32

VMEM_LIMIT_BYTES = 56 * 1024 * 1024

TM_IN = 1024
TS_CONV = 1024
CONV_HALO = 32
CONV_ROWS = 256
TQ = 256
TK = 256
TM_OUT = 1024


def _sigmoid(x):
    return 1.0 / (1.0 + jnp.exp(-x))


def _silu(x):
    return x * _sigmoid(x)


def _inproj_kernel(x_ref, gpre_ref, w_ref, glu_ref, zc_ref, qt_ref, k_ref, vt_ref, za_ref, g_ref):
    x = x_ref[...]
    ms = jnp.mean(x * x, axis=-1, keepdims=True)
    h = (x * lax.rsqrt(ms + EPS) * gpre_ref[...]).astype(BF16)

    def proj(c0, width):
        return jnp.dot(h, w_ref[:, c0:c0 + width], preferred_element_type=F32)

    def store_transposed(t_ref, y):
        for blk in range(TM_IN // TK):
            for hd in range(N_HEADS):
                t_ref[blk, hd] = y[blk * TK:(blk + 1) * TK, hd * V_DIM:(hd + 1) * V_DIM].T.astype(BF16)

    a = proj(0, D_CONV)
    b = proj(D_CONV, D_CONV)
    glu_ref[...] = (a * _sigmoid(b)).astype(BF16)
    zc_ref[...] = _silu(proj(2 * D_CONV, D_CONV)).astype(BF16)
    c0 = 3 * D_CONV
    store_transposed(qt_ref, proj(c0, D_QK) * (LOG2E / math.sqrt(HEAD_DIM)))
    k_ref[...] = proj(c0 + D_QK, D_QK).astype(BF16)
    store_transposed(vt_ref, proj(c0 + 2 * D_QK, D_ATTN))
    za_ref[...] = _silu(proj(c0 + 2 * D_QK + D_ATTN, D_ATTN)).astype(BF16)
    c0 = c0 + 2 * D_QK + 2 * D_ATTN
    for j in range(2 * D_MODEL // 512):
        g_ref[:, j * 512:(j + 1) * 512] = _sigmoid(proj(c0 + j * 512, 512)).astype(BF16)


def _inproj(x2, gpre, w_in):
    T = x2.shape[0]
    d_in = w_in.shape[1]
    row = lambda i: (i, 0)
    fixed = lambda i: (0, 0)
    tr_shape = (T // TK, N_HEADS, V_DIM, TK)
    tr_spec = pl.BlockSpec((TM_IN // TK, N_HEADS, V_DIM, TK), lambda i: (i, 0, 0, 0))
    rows = lambda w: (pl.BlockSpec((TM_IN, w), row), jax.ShapeDtypeStruct((T, w), BF16))
    tr = (tr_spec, jax.ShapeDtypeStruct(tr_shape, BF16))
    outs = (rows(D_CONV), rows(D_CONV), tr, rows(D_QK), tr, rows(D_ATTN), rows(2 * D_MODEL))
    return pl.pallas_call(
        _inproj_kernel,
        grid=(T // TM_IN,),
        in_specs=[
            pl.BlockSpec((TM_IN, D_MODEL), row),
            pl.BlockSpec((1, D_MODEL), fixed),
            pl.BlockSpec((D_MODEL, d_in), fixed, pipeline_mode=pl.Buffered(1)),
        ],
        out_specs=[o[0] for o in outs],
        out_shape=[o[1] for o in outs],
        compiler_params=pltpu.CompilerParams(
            dimension_semantics=("arbitrary",), vmem_limit_bytes=VMEM_LIMIT_BYTES),
        name="inproj",
    )(x2, gpre, w_in)


def _conv_kernel(glu_ref, halo_ref, zc_ref, w_ref, b_ref, lng_ref, lnb_ref, o_ref, xpad_ref, acc_ref):
    i = pl.program_id(1)
    halo = jnp.where(i == 0, 0.0, halo_ref[0].astype(F32))
    cur = glu_ref[0].astype(F32)
    for c in range(D_CONV // 128):
        xpad_ref[c, 0:CONV_HALO, :] = halo[:, c * 128:(c + 1) * 128]
        xpad_ref[c, CONV_HALO:, :] = cur[:, c * 128:(c + 1) * 128]

    shift = CONV_HALO - (CONV_WIDTH - 1)

    def rows(r, carry):
        r0 = pl.multiple_of(r * CONV_ROWS, CONV_ROWS)
        for c in range(D_CONV // 128):
            acc = jnp.zeros((CONV_ROWS, 128), F32)
            for j in range(CONV_WIDTH):
                xs = xpad_ref[c, pl.ds(r0 + (shift + j), CONV_ROWS), :]
                acc = acc + xs * w_ref[j:j + 1, c * 128:(c + 1) * 128]
            acc_ref[pl.ds(r0, CONV_ROWS), c * 128:(c + 1) * 128] = acc
        return carry

    lax.fori_loop(0, TS_CONV // CONV_ROWS, rows, 0)

    y = acc_ref[...] + b_ref[...]
    mu = jnp.mean(y, axis=-1, keepdims=True)
    yc = y - mu
    var = jnp.mean(yc * yc, axis=-1, keepdims=True)
    yn = yc * lax.rsqrt(var + EPS) * lng_ref[...] + lnb_ref[...]
    o_ref[0] = (_silu(yn) * zc_ref[0].astype(F32)).astype(BF16)


def _conv(glu, zc, conv_w, conv_b, ln_g, ln_b):
    B, S, _ = glu.shape
    blocks_per_halo = TS_CONV // CONV_HALO
    cur = lambda b, i: (b, i, 0)
    prev = lambda b, i: (b, jnp.maximum(i * blocks_per_halo - 1, 0), 0)
    fixed = lambda b, i: (0, 0)
    return pl.pallas_call(
        _conv_kernel,
        grid=(B, S // TS_CONV),
        in_specs=[
            pl.BlockSpec((1, TS_CONV, D_CONV), cur),
            pl.BlockSpec((1, CONV_HALO, D_CONV), prev),
            pl.BlockSpec((1, TS_CONV, D_CONV), cur),
            pl.BlockSpec((CONV_WIDTH, D_CONV), fixed),
            pl.BlockSpec((1, D_CONV), fixed),
            pl.BlockSpec((1, D_CONV), fixed),
            pl.BlockSpec((1, D_CONV), fixed),
        ],
        out_specs=pl.BlockSpec((1, TS_CONV, D_CONV), cur),
        out_shape=jax.ShapeDtypeStruct((B, S, D_CONV), BF16),
        scratch_shapes=[
            pltpu.VMEM((D_CONV // 128, TS_CONV + CONV_HALO, 128), F32),
            pltpu.VMEM((TS_CONV, D_CONV), F32),
        ],
        compiler_params=pltpu.CompilerParams(
            dimension_semantics=("arbitrary", "arbitrary"), vmem_limit_bytes=VMEM_LIMIT_BYTES),
        name="conv",
    )(glu, glu, zc, conv_w, conv_b, ln_g, ln_b)


def _bf16_round(x):
    bits = struct.unpack("<I", struct.pack("<f", x))[0]
    bits = (bits + 0x7FFF + ((bits >> 16) & 1)) & 0xFFFF0000
    return struct.unpack("<f", struct.pack("<I", bits))[0]


def _bf16_split3(c):
    c1 = _bf16_round(c)
    c2 = _bf16_round(c - c1)
    c3 = _bf16_round(c - c1 - c2)
    return c1, c2, c3


def _attn_kernel(qt_ref, k_ref, vt_ref, za_ref, lq1_ref, lk1_ref, lq2_ref, lk2_ref, subg_ref,
                 o_ref, qaug_ref, e_ref, tile_ref, acc_ref, m_ref, s_ref, smax_ref, p_ref):
    i = pl.program_id(1)
    t0 = i * TQ
    slopes2 = [s * LOG2E for s in SLOPES]

    @pl.when((pl.program_id(0) == 0) & (i == 0))
    def _():
        sl = lax.broadcasted_iota(jnp.int32, (TK, 2 * TQ), 0)
        tl = lax.broadcasted_iota(jnp.int32, (TK, 2 * TQ), 1) % TQ
        allowed = (sl // CHUNK) <= (tl // CHUNK)
        rel = (tl - jnp.abs(tl - sl) - sl).astype(F32)
        crow = lax.broadcasted_iota(jnp.int32, (2 * HEAD_DIM, 2 * TQ), 0)
        for h in range(N_HEADS):
            tile_ref[h] = jnp.where(allowed, slopes2[h] * rel, -jnp.inf)
            c1, c2, c3 = _bf16_split3(slopes2[h])
            consts = jnp.where(crow == 0, c1, jnp.where(crow == 1, c2, jnp.where(crow == 2, c3, 0.0)))
            qaug_ref[h, 2 * HEAD_DIM:, :] = consts.astype(BF16)
        lane = lax.broadcasted_iota(jnp.int32, (TK, 2 * HEAD_DIM), 1)
        row = lax.broadcasted_iota(jnp.int32, (TK, 2 * HEAD_DIM), 0)
        e_ref[...] = jnp.where(lane < 3, row, 0).astype(F32).astype(BF16)

    zeros = jnp.zeros((HEAD_DIM, TQ), BF16)
    for h in range(N_HEADS):
        qt = qt_ref[0, h]
        qaug_ref[h, 0:2 * HEAD_DIM, :] = jnp.concatenate(
            [jnp.concatenate([qt[:HEAD_DIM], zeros], axis=0),
             jnp.concatenate([zeros, qt[HEAD_DIM:]], axis=0)], axis=1)

    ones = jnp.ones((16, TK), BF16)
    ones2 = jnp.ones((16, 2 * TK), BF16)

    def scores(h, j):
        j0 = pl.multiple_of(j * TK, TK)
        lhs = jnp.concatenate([k_ref[0, pl.ds(j0, TK), h * V_DIM:(h + 1) * V_DIM], e_ref[...]], axis=1)
        return jnp.dot(lhs, qaug_ref[h], preferred_element_type=F32)

    def colmax(s):
        return jnp.max(s, axis=0, keepdims=True)

    def shift_of(h, j):
        return slopes2[h] * jnp.full((1, 2 * TQ), j * TK - t0, jnp.int32).astype(F32)

    def update_pair(h, ja, slot):
        sh_a, sh_b = shift_of(h, ja), shift_of(h, ja + 1)
        m_old = m_ref[h]
        m_new = jnp.maximum(m_old, jnp.maximum(smax_ref[slot, 0, h] + sh_a, smax_ref[slot, 1, h] + sh_b))
        p = jnp.concatenate([jnp.exp2(s_ref[slot, 0, h] - (m_new - sh_a)).astype(BF16),
                             jnp.exp2(s_ref[slot, 1, h] - (m_new - sh_b)).astype(BF16)], axis=0)
        m_ref[h] = m_new
        lhs_v = jnp.concatenate([jnp.concatenate([vt_ref[ja, h], vt_ref[ja + 1, h]], axis=1), ones2], axis=0)
        pv = jnp.dot(lhs_v, p, preferred_element_type=F32)
        acc_ref[h] = acc_ref[h] * jnp.exp2(m_old - m_new) + pv

    def update_single(h, j, slot):
        sh = shift_of(h, j)
        m_old = m_ref[h]
        m_new = jnp.maximum(m_old, smax_ref[slot, 0, h] + sh)
        p = jnp.exp2(s_ref[slot, 0, h] - (m_new - sh)).astype(BF16)
        m_ref[h] = m_new
        lhs_v = jnp.concatenate([vt_ref[j, h], ones], axis=0)
        pv = jnp.dot(lhs_v, p, preferred_element_type=F32)
        acc_ref[h] = acc_ref[h] * jnp.exp2(m_old - m_new) + pv

    n_blocks = k_ref.shape[1] // TK
    n_pairs = i // 2
    for h in range(N_HEADS):
        s_cur = scores(h, i) + tile_ref[h]
        m_cur = colmax(s_cur)
        s_a = scores(h, 0)
        s_b = scores(h, 1)
        p_ref[h] = jnp.exp2(s_cur - m_cur).astype(BF16)
        m_ref[h] = m_cur
        s_ref[0, 0, h] = s_a
        smax_ref[0, 0, h] = colmax(s_a)
        s_ref[0, 1, h] = s_b
        smax_ref[0, 1, h] = colmax(s_b)
    for h in range(N_HEADS):
        lhs_v = jnp.concatenate([vt_ref[i, h], ones], axis=0)
        acc_ref[h] = jnp.dot(lhs_v, p_ref[h], preferred_element_type=F32)

    def pair_step(jp, rd, wr):
        ja = 2 * jp
        na = jnp.minimum(ja + 2, n_blocks - 1)
        nb = jnp.minimum(ja + 3, n_blocks - 1)
        for h in range(N_HEADS):
            s_a = scores(h, na)
            s_ref[wr, 0, h] = s_a
            smax_ref[wr, 0, h] = colmax(s_a)
            s_b = scores(h, nb)
            s_ref[wr, 1, h] = s_b
            smax_ref[wr, 1, h] = colmax(s_b)
            update_pair(h, ja, rd)

    def two_pairs(jq, carry):
        pair_step(2 * jq, 0, 1)
        pair_step(2 * jq + 1, 1, 0)
        return carry

    def last_pair(_, carry):
        pair_step(n_pairs - 1, 0, 1)
        return carry

    def single(slot):
        def body(_, carry):
            for h in range(N_HEADS):
                update_single(h, i - 1, slot)
            return carry
        return body

    odd_pairs = n_pairs % 2
    lax.fori_loop(0, n_pairs // 2, two_pairs, 0)
    lax.fori_loop(0, odd_pairs, last_pair, 0)
    lax.fori_loop(0, (i % 2) * (1 - odd_pairs), single(0), 0)
    lax.fori_loop(0, (i % 2) * odd_pairs, single(1), 0)
    lam = (jnp.exp(jnp.sum(lq1_ref[...] * lk1_ref[...], axis=-1, keepdims=True))
           - jnp.exp(jnp.sum(lq2_ref[...] * lk2_ref[...], axis=-1, keepdims=True))
           + LAMBDA_INIT)
    for h in range(N_HEADS):
        hs = slice(h * V_DIM, (h + 1) * V_DIM)
        a = acc_ref[h]
        r = 1.0 / a[V_DIM:V_DIM + 1, :]
        ot = a[:V_DIM, :TQ] * r[:, :TQ] - lam * (a[:V_DIM, TQ:] * r[:, TQ:])
        ot = ot * lax.rsqrt(jnp.mean(ot * ot, axis=0, keepdims=True) + EPS)
        o = ot.T * (subg_ref[...] * (1.0 - LAMBDA_INIT))
        o_ref[0, :, hs] = (o * za_ref[0, :, hs].astype(F32)).astype(BF16)


def _attn(qt, k, vt, za, lq1, lk1, lq2, lk2, subg):
    B, S, _ = k.shape
    nq = S // TQ
    blk = lambda b, i: (b, i, 0)
    whole = lambda b, i: (b, 0, 0)
    fixed = lambda b, i: (0, 0)
    return pl.pallas_call(
        _attn_kernel,
        grid=(B, nq),
        in_specs=[
            pl.BlockSpec((1, N_HEADS, 2 * HEAD_DIM, TQ), lambda b, i: (b * nq + i, 0, 0, 0)),
            pl.BlockSpec((1, S, D_QK), whole),
            pl.BlockSpec((S // TK, N_HEADS, V_DIM, TK), lambda b, i: (b, 0, 0, 0)),
            pl.BlockSpec((1, TQ, D_ATTN), blk),
            pl.BlockSpec((1, HEAD_DIM), fixed),
            pl.BlockSpec((1, HEAD_DIM), fixed),
            pl.BlockSpec((1, HEAD_DIM), fixed),
            pl.BlockSpec((1, HEAD_DIM), fixed),
            pl.BlockSpec((1, V_DIM), fixed),
        ],
        out_specs=pl.BlockSpec((1, TQ, D_ATTN), blk),
        out_shape=jax.ShapeDtypeStruct((B, S, D_ATTN), BF16),
        scratch_shapes=[
            pltpu.VMEM((N_HEADS, 4 * HEAD_DIM, 2 * TQ), BF16),
            pltpu.VMEM((TK, 2 * HEAD_DIM), BF16),
            pltpu.VMEM((N_HEADS, TK, 2 * TQ), F32),
            pltpu.VMEM((N_HEADS, V_DIM + 16, 2 * TQ), F32),
            pltpu.VMEM((N_HEADS, 1, 2 * TQ), F32),
            pltpu.VMEM((2, 2, N_HEADS, TK, 2 * TQ), F32),
            pltpu.VMEM((2, 2, N_HEADS, 1, 2 * TQ), F32),
            pltpu.VMEM((N_HEADS, TK, 2 * TQ), BF16),
        ],
        compiler_params=pltpu.CompilerParams(
            dimension_semantics=("arbitrary", "arbitrary"), vmem_limit_bytes=VMEM_LIMIT_BYTES),
        name="attn",
    )(qt, k, vt, za, lq1, lk1, lq2, lk2, subg)


def _out_kernel(x_ref, hc_ref, ha_ref, g_ref, wc_ref, wa_ref, wo_ref, gpost_ref, o_ref):
    yc = jnp.dot(hc_ref[...], wc_ref[...], preferred_element_type=F32)
    ya = jnp.dot(ha_ref[...], wa_ref[...], preferred_element_type=F32)
    mixed = (g_ref[:, :D_MODEL].astype(F32) * yc + g_ref[:, D_MODEL:].astype(F32) * ya).astype(BF16)
    out = jnp.dot(mixed, wo_ref[...], preferred_element_type=F32)
    ms = jnp.mean(out * out, axis=-1, keepdims=True)
    o_ref[...] = x_ref[...] + out * lax.rsqrt(ms + EPS) * gpost_ref[...]


def _out(x2, hc, ha, g, wc, wa, wo, gpost):
    T = x2.shape[0]
    row = lambda i: (i, 0)
    fixed = lambda i: (0, 0)
    return pl.pallas_call(
        _out_kernel,
        grid=(T // TM_OUT,),
        in_specs=[
            pl.BlockSpec((TM_OUT, D_MODEL), row),
            pl.BlockSpec((TM_OUT, D_CONV), row),
            pl.BlockSpec((TM_OUT, D_ATTN), row),
            pl.BlockSpec((TM_OUT, 2 * D_MODEL), row),
            pl.BlockSpec((D_CONV, D_MODEL), fixed),
            pl.BlockSpec((D_ATTN, D_MODEL), fixed),
            pl.BlockSpec((D_MODEL, D_MODEL), fixed),
            pl.BlockSpec((1, D_MODEL), fixed),
        ],
        out_specs=pl.BlockSpec((TM_OUT, D_MODEL), row),
        out_shape=jax.ShapeDtypeStruct((T, D_MODEL), F32),
        compiler_params=pltpu.CompilerParams(
            dimension_semantics=("arbitrary",), vmem_limit_bytes=VMEM_LIMIT_BYTES),
        name="out",
    )(x2, hc, ha, g, wc, wa, wo, gpost)


def kernel(x, w_in, conv_w, conv_b, conv_ln_g, conv_ln_b, w_conv_proj, lambda_q1, lambda_k1,
           lambda_q2, lambda_k2, subln_g, w_attn_proj, w_out, norm_pre_g, norm_post_g):
    B, S, D = x.shape
    assert (D, w_in.shape[0]) == (D_MODEL, 1) and TQ == TK
    assert S % max(TQ, TS_CONV) == 0 and (B * S) % TM_IN == 0 and S // TK >= 2
    T = B * S
    x2 = x.reshape(T, D)
    glu, zc, qt, k, vt, za, g = _inproj(x2, norm_pre_g, w_in[0].astype(BF16))
    seq = lambda a: a.reshape(B, S, a.shape[-1])
    hc = _conv(seq(glu), seq(zc), conv_w[0], conv_b, conv_ln_g, conv_ln_b)
    ha = _attn(qt, seq(k), vt, seq(za), lambda_q1, lambda_k1, lambda_q2, lambda_k2, subln_g)
    y = _out(x2, hc.reshape(T, D_CONV), ha.reshape(T, D_ATTN), g,
             w_conv_proj[0].astype(BF16), w_attn_proj[0].astype(BF16), w_out[0].astype(BF16),
             norm_post_g)
    return y.reshape(B, S, D)
```

```python
import math
import struct

import jax
import jax.numpy as jnp
from jax import lax
from jax.experimental import pallas as pl
from jax.experimental.pallas import tpu as pltpu

D_MODEL = 1024
CHUNK = 64
CONV_WIDTH = 31
D_CONV = 512
N_HEADS = 4
HEAD_DIM = 64
V_DIM = 128
D_ATTN = 512
D_QK = 512
EPS = 1e-6
LAMBDA_INIT = 0.8 - 0.6 * math.exp(-0.3 * 0)
LOG2E = math.log2(math.e)
SLOPES = tuple(2.0 ** (-8.0 * (h + 1) / N_HEADS) for h in range(N_HEADS))

BF16 = jnp.bfloat16
F32 = jnp.float32

VMEM_LIMIT_BYTES = 56 * 1024 * 1024

TM_IN = 1024
TS_CONV = 1024
CONV_HALO = 32
CONV_ROWS = 256
TQ = 256
TK = 256
QB_PER_STEP = 2
TM_OUT = 1024


def _sigmoid(x):
    return 1.0 / (1.0 + jnp.exp(-x))


def _silu(x):
    return x * _sigmoid(x)


def _inproj_kernel(x_ref, gpre_ref, w_ref, glu_ref, zc_ref, qt_ref, k_ref, vt_ref, za_ref, g_ref):
    x = x_ref[...]
    ms = jnp.mean(x * x, axis=-1, keepdims=True)
    h = (x * lax.rsqrt(ms + EPS) * gpre_ref[...]).astype(BF16)

    def proj(c0, width):
        return jnp.dot(h, w_ref[:, c0:c0 + width], preferred_element_type=F32)

    def store_transposed(t_ref, y):
        for blk in range(TM_IN // TK):
            for hd in range(N_HEADS):
                t_ref[blk, hd] = y[blk * TK:(blk + 1) * TK, hd * V_DIM:(hd + 1) * V_DIM].T.astype(BF16)

    a = proj(0, D_CONV)
    b = proj(D_CONV, D_CONV)
    glu_ref[...] = (a * _sigmoid(b)).astype(BF16)
    zc_ref[...] = _silu(proj(2 * D_CONV, D_CONV)).astype(BF16)
    c0 = 3 * D_CONV
    store_transposed(qt_ref, proj(c0, D_QK) * (LOG2E / math.sqrt(HEAD_DIM)))
    k_ref[...] = proj(c0 + D_QK, D_QK).astype(BF16)
    store_transposed(vt_ref, proj(c0 + 2 * D_QK, D_ATTN))
    za_ref[...] = _silu(proj(c0 + 2 * D_QK + D_ATTN, D_ATTN)).astype(BF16)
    c0 = c0 + 2 * D_QK + 2 * D_ATTN
    for j in range(2 * D_MODEL // 512):
        g_ref[:, j * 512:(j + 1) * 512] = _sigmoid(proj(c0 + j * 512, 512)).astype(BF16)


def _inproj(x2, gpre, w_in):
    T = x2.shape[0]
    d_in = w_in.shape[1]
    row = lambda i: (i, 0)
    fixed = lambda i: (0, 0)
    tr_shape = (T // TK, N_HEADS, V_DIM, TK)
    tr_spec = pl.BlockSpec((TM_IN // TK, N_HEADS, V_DIM, TK), lambda i: (i, 0, 0, 0))
    rows = lambda w: (pl.BlockSpec((TM_IN, w), row), jax.ShapeDtypeStruct((T, w), BF16))
    tr = (tr_spec, jax.ShapeDtypeStruct(tr_shape, BF16))
    outs = (rows(D_CONV), rows(D_CONV), tr, rows(D_QK), tr, rows(D_ATTN), rows(2 * D_MODEL))
    return pl.pallas_call(
        _inproj_kernel,
        grid=(T // TM_IN,),
        in_specs=[
            pl.BlockSpec((TM_IN, D_MODEL), row),
            pl.BlockSpec((1, D_MODEL), fixed),
            pl.BlockSpec((D_MODEL, d_in), fixed, pipeline_mode=pl.Buffered(1)),
        ],
        out_specs=[o[0] for o in outs],
        out_shape=[o[1] for o in outs],
        compiler_params=pltpu.CompilerParams(
            dimension_semantics=("arbitrary",), vmem_limit_bytes=VMEM_LIMIT_BYTES),
        name="inproj",
    )(x2, gpre, w_in)


def _conv_kernel(glu_ref, halo_ref, zc_ref, w_ref, b_ref, lng_ref, lnb_ref, o_ref, xpad_ref, acc_ref):
    i = pl.program_id(1)
    halo = jnp.where(i == 0, 0.0, halo_ref[0].astype(F32))
    cur = glu_ref[0].astype(F32)
    for c in range(D_CONV // 128):
        xpad_ref[c, 0:CONV_HALO, :] = halo[:, c * 128:(c + 1) * 128]
        xpad_ref[c, CONV_HALO:, :] = cur[:, c * 128:(c + 1) * 128]

    shift = CONV_HALO - (CONV_WIDTH - 1)

    def rows(r, carry):
        r0 = pl.multiple_of(r * CONV_ROWS, CONV_ROWS)
        for c in range(D_CONV // 128):
            acc = jnp.zeros((CONV_ROWS, 128), F32)
            for j in range(CONV_WIDTH):
                xs = xpad_ref[c, pl.ds(r0 + (shift + j), CONV_ROWS), :]
                acc = acc + xs * w_ref[j:j + 1, c * 128:(c + 1) * 128]
            acc_ref[pl.ds(r0, CONV_ROWS), c * 128:(c + 1) * 128] = acc
        return carry

    lax.fori_loop(0, TS_CONV // CONV_ROWS, rows, 0)

    y = acc_ref[...] + b_ref[...]
    mu = jnp.mean(y, axis=-1, keepdims=True)
    yc = y - mu
    var = jnp.mean(yc * yc, axis=-1, keepdims=True)
    yn = yc * lax.rsqrt(var + EPS) * lng_ref[...] + lnb_ref[...]
    o_ref[0] = (_silu(yn) * zc_ref[0].astype(F32)).astype(BF16)


def _conv(glu, zc, conv_w, conv_b, ln_g, ln_b):
    B, S, _ = glu.shape
    blocks_per_halo = TS_CONV // CONV_HALO
    cur = lambda b, i: (b, i, 0)
    prev = lambda b, i: (b, jnp.maximum(i * blocks_per_halo - 1, 0), 0)
    fixed = lambda b, i: (0, 0)
    return pl.pallas_call(
        _conv_kernel,
        grid=(B, S // TS_CONV),
        in_specs=[
            pl.BlockSpec((1, TS_CONV, D_CONV), cur),
            pl.BlockSpec((1, CONV_HALO, D_CONV), prev),
            pl.BlockSpec((1, TS_CONV, D_CONV), cur),
            pl.BlockSpec((CONV_WIDTH, D_CONV), fixed),
            pl.BlockSpec((1, D_CONV), fixed),
            pl.BlockSpec((1, D_CONV), fixed),
            pl.BlockSpec((1, D_CONV), fixed),
        ],
        out_specs=pl.BlockSpec((1, TS_CONV, D_CONV), cur),
        out_shape=jax.ShapeDtypeStruct((B, S, D_CONV), BF16),
        scratch_shapes=[
            pltpu.VMEM((D_CONV // 128, TS_CONV + CONV_HALO, 128), F32),
            pltpu.VMEM((TS_CONV, D_CONV), F32),
        ],
        compiler_params=pltpu.CompilerParams(
            dimension_semantics=("arbitrary", "arbitrary"), vmem_limit_bytes=VMEM_LIMIT_BYTES),
        name="conv",
    )(glu, glu, zc, conv_w, conv_b, ln_g, ln_b)


def _bf16_round(x):
    bits = struct.unpack("<I", struct.pack("<f", x))[0]
    bits = (bits + 0x7FFF + ((bits >> 16) & 1)) & 0xFFFF0000
    return struct.unpack("<f", struct.pack("<I", bits))[0]


def _bf16_split3(c):
    c1 = _bf16_round(c)
    c2 = _bf16_round(c - c1)
    c3 = _bf16_round(c - c1 - c2)
    return c1, c2, c3


def _attn_kernel(qt_ref, k_ref, vt_ref, za_ref, lq1_ref, lk1_ref, lq2_ref, lk2_ref, subg_ref,
                 o_ref, qaug_ref, e_ref, tile_ref, acc_ref, m_ref, s_ref, smax_ref, p_ref):
    step = pl.program_id(1)
    slopes2 = [s * LOG2E for s in SLOPES]
    n_blocks = k_ref.shape[1] // TK

    @pl.when((pl.program_id(0) == 0) & (step == 0))
    def _():
        sl = lax.broadcasted_iota(jnp.int32, (TK, 2 * TQ), 0)
        tl = lax.broadcasted_iota(jnp.int32, (TK, 2 * TQ), 1) % TQ
        allowed = (sl // CHUNK) <= (tl // CHUNK)
        rel = (tl - jnp.abs(tl - sl) - sl).astype(F32)
        crow = lax.broadcasted_iota(jnp.int32, (2 * HEAD_DIM, 2 * TQ), 0)
        for h in range(N_HEADS):
            tile_ref[h] = jnp.where(allowed, slopes2[h] * rel, -jnp.inf)
            c1, c2, c3 = _bf16_split3(slopes2[h])
            consts = jnp.where(crow == 0, c1, jnp.where(crow == 1, c2, jnp.where(crow == 2, c3, 0.0)))
            qaug_ref[h, 2 * HEAD_DIM:, :] = consts.astype(BF16)
        lane = lax.broadcasted_iota(jnp.int32, (TK, 2 * HEAD_DIM), 1)
        row = lax.broadcasted_iota(jnp.int32, (TK, 2 * HEAD_DIM), 0)
        e_ref[...] = jnp.where(lane < 3, row, 0).astype(F32).astype(BF16)

    zeros = jnp.zeros((HEAD_DIM, TQ), BF16)
    ones = jnp.ones((16, TK), BF16)
    ones2 = jnp.ones((16, 2 * TK), BF16)

    def colmax(s):
        return jnp.max(s, axis=0, keepdims=True)

    def scores(h, j):
        j0 = pl.multiple_of(j * TK, TK)
        lhs = jnp.concatenate([k_ref[0, pl.ds(j0, TK), h * V_DIM:(h + 1) * V_DIM], e_ref[...]], axis=1)
        return jnp.dot(lhs, qaug_ref[h], preferred_element_type=F32)

    def first_stage(sub, i):
        for h in range(N_HEADS):
            qt = qt_ref[sub, h]
            qaug_ref[h, 0:2 * HEAD_DIM, :] = jnp.concatenate(
                [jnp.concatenate([qt[:HEAD_DIM], zeros], axis=0),
                 jnp.concatenate([zeros, qt[HEAD_DIM:]], axis=0)], axis=1)
        for h in range(N_HEADS):
            s_cur = scores(h, i) + tile_ref[h]
            m_cur = colmax(s_cur)
            s_a = scores(h, 0)
            s_b = scores(h, 1)
            p_ref[h] = jnp.exp2(s_cur - m_cur).astype(BF16)
            m_ref[h] = m_cur
            s_ref[0, 0, h] = s_a
            smax_ref[0, 0, h] = colmax(s_a)
            s_ref[0, 1, h] = s_b
            smax_ref[0, 1, h] = colmax(s_b)
        for h in range(N_HEADS):
            lhs_v = jnp.concatenate([vt_ref[i, h], ones], axis=0)
            acc_ref[h] = jnp.dot(lhs_v, p_ref[h], preferred_element_type=F32)

    def earlier_blocks(i):
        t0 = i * TQ
        n_pairs = i // 2
        odd_pairs = n_pairs % 2

        def shift_of(h, j):
            return slopes2[h] * jnp.full((1, 2 * TQ), j * TK - t0, jnp.int32).astype(F32)

        def update_pair(h, ja, slot):
            sh_a, sh_b = shift_of(h, ja), shift_of(h, ja + 1)
            m_old = m_ref[h]
            m_new = jnp.maximum(m_old, jnp.maximum(smax_ref[slot, 0, h] + sh_a, smax_ref[slot, 1, h] + sh_b))
            p = jnp.concatenate([jnp.exp2(s_ref[slot, 0, h] - (m_new - sh_a)).astype(BF16),
                                 jnp.exp2(s_ref[slot, 1, h] - (m_new - sh_b)).astype(BF16)], axis=0)
            m_ref[h] = m_new
            lhs_v = jnp.concatenate([jnp.concatenate([vt_ref[ja, h], vt_ref[ja + 1, h]], axis=1), ones2], axis=0)
            pv = jnp.dot(lhs_v, p, preferred_element_type=F32)
            acc_ref[h] = acc_ref[h] * jnp.exp2(m_old - m_new) + pv

        def update_single(h, j, slot):
            sh = shift_of(h, j)
            m_old = m_ref[h]
            m_new = jnp.maximum(m_old, smax_ref[slot, 0, h] + sh)
            p = jnp.exp2(s_ref[slot, 0, h] - (m_new - sh)).astype(BF16)
            m_ref[h] = m_new
            lhs_v = jnp.concatenate([vt_ref[j, h], ones], axis=0)
            pv = jnp.dot(lhs_v, p, preferred_element_type=F32)
            acc_ref[h] = acc_ref[h] * jnp.exp2(m_old - m_new) + pv

        def pair_step(jp, rd, wr):
            ja = 2 * jp
            na = jnp.minimum(ja + 2, n_blocks - 1)
            nb = jnp.minimum(ja + 3, n_blocks - 1)
            for h in range(N_HEADS):
                s_a = scores(h, na)
                s_ref[wr, 0, h] = s_a
                smax_ref[wr, 0, h] = colmax(s_a)
                s_b = scores(h, nb)
                s_ref[wr, 1, h] = s_b
                smax_ref[wr, 1, h] = colmax(s_b)
                update_pair(h, ja, rd)

        def two_pairs(jq, carry):
            pair_step(2 * jq, 0, 1)
            pair_step(2 * jq + 1, 1, 0)
            return carry

        def last_pair(_, carry):
            pair_step(n_pairs - 1, 0, 1)
            return carry

        def single(slot):
            def body(_, carry):
                for h in range(N_HEADS):
                    update_single(h, i - 1, slot)
                return carry
            return body

        lax.fori_loop(0, n_pairs // 2, two_pairs, 0)
        lax.fori_loop(0, odd_pairs, last_pair, 0)
        lax.fori_loop(0, (i % 2) * (1 - odd_pairs), single(0), 0)
        lax.fori_loop(0, (i % 2) * odd_pairs, single(1), 0)

    def last_stage(sub):
        lam = (jnp.exp(jnp.sum(lq1_ref[...] * lk1_ref[...], axis=-1, keepdims=True))
               - jnp.exp(jnp.sum(lq2_ref[...] * lk2_ref[...], axis=-1, keepdims=True))
               + LAMBDA_INIT)
        rows = slice(sub * TQ, (sub + 1) * TQ)
        for h in range(N_HEADS):
            hs = slice(h * V_DIM, (h + 1) * V_DIM)
            a = acc_ref[h]
            r = 1.0 / a[V_DIM:V_DIM + 1, :]
            ot = a[:V_DIM, :TQ] * r[:, :TQ] - lam * (a[:V_DIM, TQ:] * r[:, TQ:])
            ot = ot * lax.rsqrt(jnp.mean(ot * ot, axis=0, keepdims=True) + EPS)
            o = ot.T * (subg_ref[...] * (1.0 - LAMBDA_INIT))
            o_ref[0, rows, hs] = (o * za_ref[0, rows, hs].astype(F32)).astype(BF16)

    first_stage(0, step * QB_PER_STEP)
    for sub in range(QB_PER_STEP):
        i = step * QB_PER_STEP + sub
        earlier_blocks(i)
        last_stage(sub)
        if sub + 1 < QB_PER_STEP:
            first_stage(sub + 1, i + 1)


def _attn(qt, k, vt, za, lq1, lk1, lq2, lk2, subg):
    B, S, _ = k.shape
    nq = S // (QB_PER_STEP * TQ)
    blk = lambda b, i: (b, i, 0)
    whole = lambda b, i: (b, 0, 0)
    fixed = lambda b, i: (0, 0)
    return pl.pallas_call(
        _attn_kernel,
        grid=(B, nq),
        in_specs=[
            pl.BlockSpec((QB_PER_STEP, N_HEADS, 2 * HEAD_DIM, TQ), lambda b, i: (b * nq + i, 0, 0, 0)),
            pl.BlockSpec((1, S, D_QK), whole),
            pl.BlockSpec((S // TK, N_HEADS, V_DIM, TK), lambda b, i: (b, 0, 0, 0)),
            pl.BlockSpec((1, QB_PER_STEP * TQ, D_ATTN), blk),
            pl.BlockSpec((1, HEAD_DIM), fixed),
            pl.BlockSpec((1, HEAD_DIM), fixed),
            pl.BlockSpec((1, HEAD_DIM), fixed),
            pl.BlockSpec((1, HEAD_DIM), fixed),
            pl.BlockSpec((1, V_DIM), fixed),
        ],
        out_specs=pl.BlockSpec((1, QB_PER_STEP * TQ, D_ATTN), blk),
        out_shape=jax.ShapeDtypeStruct((B, S, D_ATTN), BF16),
        scratch_shapes=[
            pltpu.VMEM((N_HEADS, 4 * HEAD_DIM, 2 * TQ), BF16),
            pltpu.VMEM((TK, 2 * HEAD_DIM), BF16),
            pltpu.VMEM((N_HEADS, TK, 2 * TQ), F32),
            pltpu.VMEM((N_HEADS, V_DIM + 16, 2 * TQ), F32),
            pltpu.VMEM((N_HEADS, 1, 2 * TQ), F32),
            pltpu.VMEM((2, 2, N_HEADS, TK, 2 * TQ), F32),
            pltpu.VMEM((2, 2, N_HEADS, 1, 2 * TQ), F32),
            pltpu.VMEM((N_HEADS, TK, 2 * TQ), BF16),
        ],
        compiler_params=pltpu.CompilerParams(
            dimension_semantics=("arbitrary", "arbitrary"), vmem_limit_bytes=VMEM_LIMIT_BYTES),
        name="attn",
    )(qt, k, vt, za, lq1, lk1, lq2, lk2, subg)


def _out_kernel(x_ref, hc_ref, ha_ref, g_ref, wc_ref, wa_ref, wo_ref, gpost_ref, o_ref):
    yc = jnp.dot(hc_ref[...], wc_ref[...], preferred_element_type=F32)
    ya = jnp.dot(ha_ref[...], wa_ref[...], preferred_element_type=F32)
    mixed = (g_ref[:, :D_MODEL].astype(F32) * yc + g_ref[:, D_MODEL:].astype(F32) * ya).astype(BF16)
    out = jnp.dot(mixed, wo_ref[...], preferred_element_type=F32)
    ms = jnp.mean(out * out, axis=-1, keepdims=True)
    o_ref[...] = x_ref[...] + out * lax.rsqrt(ms + EPS) * gpost_ref[...]


def _out(x2, hc, ha, g, wc, wa, wo, gpost):
    T = x2.shape[0]
    row = lambda i: (i, 0)
    fixed = lambda i: (0, 0)
    return pl.pallas_call(
        _out_kernel,
        grid=(T // TM_OUT,),
        in_specs=[
            pl.BlockSpec((TM_OUT, D_MODEL), row),
            pl.BlockSpec((TM_OUT, D_CONV), row),
            pl.BlockSpec((TM_OUT, D_ATTN), row),
            pl.BlockSpec((TM_OUT, 2 * D_MODEL), row),
            pl.BlockSpec((D_CONV, D_MODEL), fixed),
            pl.BlockSpec((D_ATTN, D_MODEL), fixed),
            pl.BlockSpec((D_MODEL, D_MODEL), fixed),
            pl.BlockSpec((1, D_MODEL), fixed),
        ],
        out_specs=pl.BlockSpec((TM_OUT, D_MODEL), row),
        out_shape=jax.ShapeDtypeStruct((T, D_MODEL), F32),
        compiler_params=pltpu.CompilerParams(
            dimension_semantics=("arbitrary",), vmem_limit_bytes=VMEM_LIMIT_BYTES),
        name="out",
    )(x2, hc, ha, g, wc, wa, wo, gpost)


def kernel(x, w_in, conv_w, conv_b, conv_ln_g, conv_ln_b, w_conv_proj, lambda_q1, lambda_k1,
           lambda_q2, lambda_k2, subln_g, w_attn_proj, w_out, norm_pre_g, norm_post_g):
    B, S, D = x.shape
    assert (D, w_in.shape[0]) == (D_MODEL, 1) and TQ == TK
    assert S % max(QB_PER_STEP * TQ, TS_CONV) == 0 and (B * S) % TM_IN == 0 and S // TK >= 2
    T = B * S
    x2 = x.reshape(T, D)
    glu, zc, qt, k, vt, za, g = _inproj(x2, norm_pre_g, w_in[0].astype(BF16))
    seq = lambda a: a.reshape(B, S, a.shape[-1])
    hc = _conv(seq(glu), seq(zc), conv_w[0], conv_b, conv_ln_g, conv_ln_b)
    ha = _attn(qt, seq(k), vt, seq(za), lambda_q1, lambda_k1, lambda_q2, lambda_k2, subln_g)
    y = _out(x2, hc.reshape(T, D_CONV), ha.reshape(T, D_ATTN), g,
             w_conv_proj[0].astype(BF16), w_attn_proj[0].astype(BF16), w_out[0].astype(BF16),
             norm_post_g)
    return y.reshape(B, S, D)
```

```python
import math
import struct

import jax
import jax.numpy as jnp
from jax import lax
from jax.experimental import pallas as pl
from jax.experimental.pallas import tpu as pltpu

D_MODEL = 1024
CHUNK = 64
CONV_WIDTH = 31
D_CONV = 512
N_HEADS = 4
HEAD_DIM = 64
V_DIM = 128
D_ATTN = 512
D_QK = 512
EPS = 1e-6
LAMBDA_INIT = 0.8 - 0.6 * math.exp(-0.3 * 0)
LOG2E = math.log2(math.e)
SLOPES = tuple(2.0 ** (-8.0 * (h + 1) / N_HEADS) for h in range(N_HEADS))

BF16 = jnp.bfloat16
F32 = jnp.float32

VMEM_LIMIT_BYTES = 56 * 1024 * 1024

TM_IN = 1024
TS_CONV = 1024
CONV_HALO = 32
CONV_ROWS = 256
TQ = 256
TK = 256
QB_PER_STEP = 4
TM_OUT = 1024


def _sigmoid(x):
    return 1.0 / (1.0 + jnp.exp(-x))


def _silu(x):
    return x * _sigmoid(x)


def _inproj_kernel(x_ref, gpre_ref, w_ref, glu_ref, zc_ref, qt_ref, k_ref, vt_ref, za_ref, g_ref):
    x = x_ref[...]
    ms = jnp.mean(x * x, axis=-1, keepdims=True)
    h = (x * lax.rsqrt(ms + EPS) * gpre_ref[...]).astype(BF16)

    def proj(c0, width):
        return jnp.dot(h, w_ref[:, c0:c0 + width], preferred_element_type=F32)

    def store_transposed(t_ref, y):
        for blk in range(TM_IN // TK):
            for hd in range(N_HEADS):
                t_ref[blk, hd] = y[blk * TK:(blk + 1) * TK, hd * V_DIM:(hd + 1) * V_DIM].T.astype(BF16)

    a = proj(0, D_CONV)
    b = proj(D_CONV, D_CONV)
    glu_ref[...] = (a * _sigmoid(b)).astype(BF16)
    zc_ref[...] = _silu(proj(2 * D_CONV, D_CONV)).astype(BF16)
    c0 = 3 * D_CONV
    store_transposed(qt_ref, proj(c0, D_QK) * (LOG2E / math.sqrt(HEAD_DIM)))
    k_ref[...] = proj(c0 + D_QK, D_QK).astype(BF16)
    store_transposed(vt_ref, proj(c0 + 2 * D_QK, D_ATTN))
    za_ref[...] = _silu(proj(c0 + 2 * D_QK + D_ATTN, D_ATTN)).astype(BF16)
    c0 = c0 + 2 * D_QK + 2 * D_ATTN
    for j in range(2 * D_MODEL // 512):
        g_ref[:, j * 512:(j + 1) * 512] = _sigmoid(proj(c0 + j * 512, 512)).astype(BF16)


def _inproj(x2, gpre, w_in):
    T = x2.shape[0]
    d_in = w_in.shape[1]
    row = lambda i: (i, 0)
    fixed = lambda i: (0, 0)
    tr_shape = (T // TK, N_HEADS, V_DIM, TK)
    tr_spec = pl.BlockSpec((TM_IN // TK, N_HEADS, V_DIM, TK), lambda i: (i, 0, 0, 0))
    rows = lambda w: (pl.BlockSpec((TM_IN, w), row), jax.ShapeDtypeStruct((T, w), BF16))
    tr = (tr_spec, jax.ShapeDtypeStruct(tr_shape, BF16))
    outs = (rows(D_CONV), rows(D_CONV), tr, rows(D_QK), tr, rows(D_ATTN), rows(2 * D_MODEL))
    return pl.pallas_call(
        _inproj_kernel,
        grid=(T // TM_IN,),
        in_specs=[
            pl.BlockSpec((TM_IN, D_MODEL), row),
            pl.BlockSpec((1, D_MODEL), fixed),
            pl.BlockSpec((D_MODEL, d_in), fixed, pipeline_mode=pl.Buffered(1)),
        ],
        out_specs=[o[0] for o in outs],
        out_shape=[o[1] for o in outs],
        compiler_params=pltpu.CompilerParams(
            dimension_semantics=("arbitrary",), vmem_limit_bytes=VMEM_LIMIT_BYTES),
        name="inproj",
    )(x2, gpre, w_in)


def _conv_kernel(glu_ref, halo_ref, zc_ref, w_ref, b_ref, lng_ref, lnb_ref, o_ref, xpad_ref, acc_ref):
    i = pl.program_id(1)
    halo = jnp.where(i == 0, 0.0, halo_ref[0].astype(F32))
    cur = glu_ref[0].astype(F32)
    for c in range(D_CONV // 128):
        xpad_ref[c, 0:CONV_HALO, :] = halo[:, c * 128:(c + 1) * 128]
        xpad_ref[c, CONV_HALO:, :] = cur[:, c * 128:(c + 1) * 128]

    shift = CONV_HALO - (CONV_WIDTH - 1)

    def rows(r, carry):
        r0 = pl.multiple_of(r * CONV_ROWS, CONV_ROWS)
        for c in range(D_CONV // 128):
            acc = jnp.zeros((CONV_ROWS, 128), F32)
            for j in range(CONV_WIDTH):
                xs = xpad_ref[c, pl.ds(r0 + (shift + j), CONV_ROWS), :]
                acc = acc + xs * w_ref[j:j + 1, c * 128:(c + 1) * 128]
            acc_ref[pl.ds(r0, CONV_ROWS), c * 128:(c + 1) * 128] = acc
        return carry

    lax.fori_loop(0, TS_CONV // CONV_ROWS, rows, 0)

    y = acc_ref[...] + b_ref[...]
    mu = jnp.mean(y, axis=-1, keepdims=True)
    yc = y - mu
    var = jnp.mean(yc * yc, axis=-1, keepdims=True)
    yn = yc * lax.rsqrt(var + EPS) * lng_ref[...] + lnb_ref[...]
    o_ref[0] = (_silu(yn) * zc_ref[0].astype(F32)).astype(BF16)


def _conv(glu, zc, conv_w, conv_b, ln_g, ln_b):
    B, S, _ = glu.shape
    blocks_per_halo = TS_CONV // CONV_HALO
    cur = lambda b, i: (b, i, 0)
    prev = lambda b, i: (b, jnp.maximum(i * blocks_per_halo - 1, 0), 0)
    fixed = lambda b, i: (0, 0)
    return pl.pallas_call(
        _conv_kernel,
        grid=(B, S // TS_CONV),
        in_specs=[
            pl.BlockSpec((1, TS_CONV, D_CONV), cur),
            pl.BlockSpec((1, CONV_HALO, D_CONV), prev),
            pl.BlockSpec((1, TS_CONV, D_CONV), cur),
            pl.BlockSpec((CONV_WIDTH, D_CONV), fixed),
            pl.BlockSpec((1, D_CONV), fixed),
            pl.BlockSpec((1, D_CONV), fixed),
            pl.BlockSpec((1, D_CONV), fixed),
        ],
        out_specs=pl.BlockSpec((1, TS_CONV, D_CONV), cur),
        out_shape=jax.ShapeDtypeStruct((B, S, D_CONV), BF16),
        scratch_shapes=[
            pltpu.VMEM((D_CONV // 128, TS_CONV + CONV_HALO, 128), F32),
            pltpu.VMEM((TS_CONV, D_CONV), F32),
        ],
        compiler_params=pltpu.CompilerParams(
            dimension_semantics=("arbitrary", "arbitrary"), vmem_limit_bytes=VMEM_LIMIT_BYTES),
        name="conv",
    )(glu, glu, zc, conv_w, conv_b, ln_g, ln_b)


def _bf16_round(x):
    bits = struct.unpack("<I", struct.pack("<f", x))[0]
    bits = (bits + 0x7FFF + ((bits >> 16) & 1)) & 0xFFFF0000
    return struct.unpack("<f", struct.pack("<I", bits))[0]


def _bf16_split3(c):
    c1 = _bf16_round(c)
    c2 = _bf16_round(c - c1)
    c3 = _bf16_round(c - c1 - c2)
    return c1, c2, c3


def _attn_kernel(qt_ref, k_ref, vt_ref, za_ref, lq1_ref, lk1_ref, lq2_ref, lk2_ref, subg_ref,
                 o_ref, qaug_ref, e_ref, tile_ref, acc_ref, m_ref, s_ref, smax_ref, p_ref):
    step = pl.program_id(1)
    slopes2 = [s * LOG2E for s in SLOPES]
    n_blocks = k_ref.shape[1] // TK

    @pl.when((pl.program_id(0) == 0) & (step == 0))
    def _():
        sl = lax.broadcasted_iota(jnp.int32, (TK, 2 * TQ), 0)
        tl = lax.broadcasted_iota(jnp.int32, (TK, 2 * TQ), 1) % TQ
        allowed = (sl // CHUNK) <= (tl // CHUNK)
        rel = (tl - jnp.abs(tl - sl) - sl).astype(F32)
        crow = lax.broadcasted_iota(jnp.int32, (2 * HEAD_DIM, 2 * TQ), 0)
        for h in range(N_HEADS):
            tile_ref[h] = jnp.where(allowed, slopes2[h] * rel, -jnp.inf)
            c1, c2, c3 = _bf16_split3(slopes2[h])
            consts = jnp.where(crow == 0, c1, jnp.where(crow == 1, c2, jnp.where(crow == 2, c3, 0.0)))
            qaug_ref[h, 2 * HEAD_DIM:, :] = consts.astype(BF16)
        lane = lax.broadcasted_iota(jnp.int32, (TK, 2 * HEAD_DIM), 1)
        row = lax.broadcasted_iota(jnp.int32, (TK, 2 * HEAD_DIM), 0)
        e_ref[...] = jnp.where(lane < 3, row, 0).astype(F32).astype(BF16)

    zeros = jnp.zeros((HEAD_DIM, TQ), BF16)
    ones = jnp.ones((16, TK), BF16)
    ones2 = jnp.ones((16, 2 * TK), BF16)

    def colmax(s):
        return jnp.max(s, axis=0, keepdims=True)

    def scores(h, j):
        j0 = pl.multiple_of(j * TK, TK)
        lhs = jnp.concatenate([k_ref[0, pl.ds(j0, TK), h * V_DIM:(h + 1) * V_DIM], e_ref[...]], axis=1)
        return jnp.dot(lhs, qaug_ref[h], preferred_element_type=F32)

    def first_stage(sub, i):
        for h in range(N_HEADS):
            qt = qt_ref[sub, h]
            qaug_ref[h, 0:2 * HEAD_DIM, :] = jnp.concatenate(
                [jnp.concatenate([qt[:HEAD_DIM], zeros], axis=0),
                 jnp.concatenate([zeros, qt[HEAD_DIM:]], axis=0)], axis=1)
        for h in range(N_HEADS):
            s_cur = scores(h, i) + tile_ref[h]
            m_cur = colmax(s_cur)
            s_a = scores(h, 0)
            s_b = scores(h, 1)
            p_ref[h] = jnp.exp2(s_cur - m_cur).astype(BF16)
            m_ref[h] = m_cur
            s_ref[0, 0, h] = s_a
            smax_ref[0, 0, h] = colmax(s_a)
            s_ref[0, 1, h] = s_b
            smax_ref[0, 1, h] = colmax(s_b)
        for h in range(N_HEADS):
            lhs_v = jnp.concatenate([vt_ref[i, h], ones], axis=0)
            acc_ref[h] = jnp.dot(lhs_v, p_ref[h], preferred_element_type=F32)

    def earlier_blocks(i):
        t0 = i * TQ
        n_pairs = i // 2
        odd_pairs = n_pairs % 2

        def shift_of(h, j):
            return slopes2[h] * jnp.full((1, 2 * TQ), j * TK - t0, jnp.int32).astype(F32)

        def update_pair(h, ja, slot):
            sh_a, sh_b = shift_of(h, ja), shift_of(h, ja + 1)
            m_old = m_ref[h]
            m_new = jnp.maximum(m_old, jnp.maximum(smax_ref[slot, 0, h] + sh_a, smax_ref[slot, 1, h] + sh_b))
            p = jnp.concatenate([jnp.exp2(s_ref[slot, 0, h] - (m_new - sh_a)).astype(BF16),
                                 jnp.exp2(s_ref[slot, 1, h] - (m_new - sh_b)).astype(BF16)], axis=0)
            m_ref[h] = m_new
            lhs_v = jnp.concatenate([jnp.concatenate([vt_ref[ja, h], vt_ref[ja + 1, h]], axis=1), ones2], axis=0)
            pv = jnp.dot(lhs_v, p, preferred_element_type=F32)
            acc_ref[h] = acc_ref[h] * jnp.exp2(m_old - m_new) + pv

        def update_single(h, j, slot):
            sh = shift_of(h, j)
            m_old = m_ref[h]
            m_new = jnp.maximum(m_old, smax_ref[slot, 0, h] + sh)
            p = jnp.exp2(s_ref[slot, 0, h] - (m_new - sh)).astype(BF16)
            m_ref[h] = m_new
            lhs_v = jnp.concatenate([vt_ref[j, h], ones], axis=0)
            pv = jnp.dot(lhs_v, p, preferred_element_type=F32)
            acc_ref[h] = acc_ref[h] * jnp.exp2(m_old - m_new) + pv

        def pair_step(jp, rd, wr):
            ja = 2 * jp
            na = jnp.minimum(ja + 2, n_blocks - 1)
            nb = jnp.minimum(ja + 3, n_blocks - 1)
            for h in range(N_HEADS):
                s_a = scores(h, na)
                s_ref[wr, 0, h] = s_a
                smax_ref[wr, 0, h] = colmax(s_a)
                s_b = scores(h, nb)
                s_ref[wr, 1, h] = s_b
                smax_ref[wr, 1, h] = colmax(s_b)
                update_pair(h, ja, rd)

        def two_pairs(jq, carry):
            pair_step(2 * jq, 0, 1)
            pair_step(2 * jq + 1, 1, 0)
            return carry

        def last_pair(_, carry):
            pair_step(n_pairs - 1, 0, 1)
            return carry

        def single(slot):
            def body(_, carry):
                for h in range(N_HEADS):
                    update_single(h, i - 1, slot)
                return carry
            return body

        lax.fori_loop(0, n_pairs // 2, two_pairs, 0)
        lax.fori_loop(0, odd_pairs, last_pair, 0)
        lax.fori_loop(0, (i % 2) * (1 - odd_pairs), single(0), 0)
        lax.fori_loop(0, (i % 2) * odd_pairs, single(1), 0)

    def last_stage(sub):
        lam = (jnp.exp(jnp.sum(lq1_ref[...] * lk1_ref[...], axis=-1, keepdims=True))
               - jnp.exp(jnp.sum(lq2_ref[...] * lk2_ref[...], axis=-1, keepdims=True))
               + LAMBDA_INIT)
        rows = slice(sub * TQ, (sub + 1) * TQ)
        for h in range(N_HEADS):
            hs = slice(h * V_DIM, (h + 1) * V_DIM)
            a = acc_ref[h]
            r = 1.0 / a[V_DIM:V_DIM + 1, :]
            ot = a[:V_DIM, :TQ] * r[:, :TQ] - lam * (a[:V_DIM, TQ:] * r[:, TQ:])
            ot = ot * lax.rsqrt(jnp.mean(ot * ot, axis=0, keepdims=True) + EPS)
            o = ot.T * (subg_ref[...] * (1.0 - LAMBDA_INIT))
            o_ref[0, rows, hs] = (o * za_ref[0, rows, hs].astype(F32)).astype(BF16)

    first_stage(0, step * QB_PER_STEP)
    for sub in range(QB_PER_STEP):
        i = step * QB_PER_STEP + sub
        earlier_blocks(i)
        last_stage(sub)
        if sub + 1 < QB_PER_STEP:
            first_stage(sub + 1, i + 1)


def _attn(qt, k, vt, za, lq1, lk1, lq2, lk2, subg):
    B, S, _ = k.shape
    nq = S // (QB_PER_STEP * TQ)
    blk = lambda b, i: (b, i, 0)
    whole = lambda b, i: (b, 0, 0)
    fixed = lambda b, i: (0, 0)
    return pl.pallas_call(
        _attn_kernel,
        grid=(B, nq),
        in_specs=[
            pl.BlockSpec((QB_PER_STEP, N_HEADS, 2 * HEAD_DIM, TQ), lambda b, i: (b * nq + i, 0, 0, 0)),
            pl.BlockSpec((1, S, D_QK), whole),
            pl.BlockSpec((S // TK, N_HEADS, V_DIM, TK), lambda b, i: (b, 0, 0, 0)),
            pl.BlockSpec((1, QB_PER_STEP * TQ, D_ATTN), blk),
            pl.BlockSpec((1, HEAD_DIM), fixed),
            pl.BlockSpec((1, HEAD_DIM), fixed),
            pl.BlockSpec((1, HEAD_DIM), fixed),
            pl.BlockSpec((1, HEAD_DIM), fixed),
            pl.BlockSpec((1, V_DIM), fixed),
        ],
        out_specs=pl.BlockSpec((1, QB_PER_STEP * TQ, D_ATTN), blk),
        out_shape=jax.ShapeDtypeStruct((B, S, D_ATTN), BF16),
        scratch_shapes=[
            pltpu.VMEM((N_HEADS, 4 * HEAD_DIM, 2 * TQ), BF16),
            pltpu.VMEM((TK, 2 * HEAD_DIM), BF16),
            pltpu.VMEM((N_HEADS, TK, 2 * TQ), F32),
            pltpu.VMEM((N_HEADS, V_DIM + 16, 2 * TQ), F32),
            pltpu.VMEM((N_HEADS, 1, 2 * TQ), F32),
            pltpu.VMEM((2, 2, N_HEADS, TK, 2 * TQ), F32),
            pltpu.VMEM((2, 2, N_HEADS, 1, 2 * TQ), F32),
            pltpu.VMEM((N_HEADS, TK, 2 * TQ), BF16),
        ],
        compiler_params=pltpu.CompilerParams(
            dimension_semantics=("arbitrary", "arbitrary"), vmem_limit_bytes=VMEM_LIMIT_BYTES),
        name="attn",
    )(qt, k, vt, za, lq1, lk1, lq2, lk2, subg)


def _out_kernel(x_ref, hc_ref, ha_ref, g_ref, wc_ref, wa_ref, wo_ref, gpost_ref, o_ref):
    yc = jnp.dot(hc_ref[...], wc_ref[...], preferred_element_type=F32)
    ya = jnp.dot(ha_ref[...], wa_ref[...], preferred_element_type=F32)
    mixed = (g_ref[:, :D_MODEL].astype(F32) * yc + g_ref[:, D_MODEL:].astype(F32) * ya).astype(BF16)
    out = jnp.dot(mixed, wo_ref[...], preferred_element_type=F32)
    ms = jnp.mean(out * out, axis=-1, keepdims=True)
    o_ref[...] = x_ref[...] + out * lax.rsqrt(ms + EPS) * gpost_ref[...]


def _out(x2, hc, ha, g, wc, wa, wo, gpost):
    T = x2.shape[0]
    row = lambda i: (i, 0)
    fixed = lambda i: (0, 0)
    return pl.pallas_call(
        _out_kernel,
        grid=(T // TM_OUT,),
        in_specs=[
            pl.BlockSpec((TM_OUT, D_MODEL), row),
            pl.BlockSpec((TM_OUT, D_CONV), row),
            pl.BlockSpec((TM_OUT, D_ATTN), row),
            pl.BlockSpec((TM_OUT, 2 * D_MODEL), row),
            pl.BlockSpec((D_CONV, D_MODEL), fixed),
            pl.BlockSpec((D_ATTN, D_MODEL), fixed),
            pl.BlockSpec((D_MODEL, D_MODEL), fixed),
            pl.BlockSpec((1, D_MODEL), fixed),
        ],
        out_specs=pl.BlockSpec((TM_OUT, D_MODEL), row),
        out_shape=jax.ShapeDtypeStruct((T, D_MODEL), F32),
        compiler_params=pltpu.CompilerParams(
            dimension_semantics=("arbitrary",), vmem_limit_bytes=VMEM_LIMIT_BYTES),
        name="out",
    )(x2, hc, ha, g, wc, wa, wo, gpost)


def kernel(x, w_in, conv_w, conv_b, conv_ln_g, conv_ln_b, w_conv_proj, lambda_q1, lambda_k1,
           lambda_q2, lambda_k2, subln_g, w_attn_proj, w_out, norm_pre_g, norm_post_g):
    B, S, D = x.shape
    assert (D, w_in.shape[0]) == (D_MODEL, 1) and TQ == TK
    assert S % max(QB_PER_STEP * TQ, TS_CONV) == 0 and (B * S) % TM_IN == 0 and S // TK >= 2
    T = B * S
    x2 = x.reshape(T, D)
    glu, zc, qt, k, vt, za, g = _inproj(x2, norm_pre_g, w_in[0].astype(BF16))
    seq = lambda a: a.reshape(B, S, a.shape[-1])
    hc = _conv(seq(glu), seq(zc), conv_w[0], conv_b, conv_ln_g, conv_ln_b)
    ha = _attn(qt, seq(k), vt, seq(za), lambda_q1, lambda_k1, lambda_q2, lambda_k2, subln_g)
    y = _out(x2, hc.reshape(T, D_CONV), ha.reshape(T, D_ATTN), g,
             w_conv_proj[0].astype(BF16), w_attn_proj[0].astype(BF16), w_out[0].astype(BF16),
             norm_post_g)
    return y.reshape(B, S, D)
```

```python
import math
import struct

import jax
import jax.numpy as jnp
from jax import lax
from jax.experimental import pallas as pl
from jax.experimental.pallas import tpu as pltpu

D_MODEL = 1024
CHUNK = 64
CONV_WIDTH = 31
D_CONV = 512
N_HEADS = 4
HEAD_DIM = 64
V_DIM = 128
D_ATTN = 512
D_QK = 512
EPS = 1e-6
LAMBDA_INIT = 0.8 - 0.6 * math.exp(-0.3 * 0)
LOG2E = math.log2(math.e)
SLOPES = tuple(2.0 ** (-8.0 * (h + 1) / N_HEADS) for h in range(N_HEADS))

BF16 = jnp.bfloat16
F32 = jnp.float32

VMEM_LIMIT_BYTES = 56 * 1024 * 1024

TM_IN = 1024
TS_CONV = 1024
CONV_HALO = 32
CONV_ROWS = 256
TQ = 256
TK = 256
QB_PER_STEP = 2
TM_OUT = 1024
OUT_ROWS = 256


def _sigmoid(x):
    return 1.0 / (1.0 + jnp.exp(-x))


def _silu(x):
    return x * _sigmoid(x)


def _inproj_kernel(x_ref, gpre_ref, w_ref, glu_ref, zc_ref, qt_ref, k_ref, vt_ref, za_ref, g_ref):
    x = x_ref[...]
    ms = jnp.mean(x * x, axis=-1, keepdims=True)
    h = (x * lax.rsqrt(ms + EPS) * gpre_ref[...]).astype(BF16)

    def proj(c0, width):
        return jnp.dot(h, w_ref[:, c0:c0 + width], preferred_element_type=F32)

    def store_transposed(t_ref, y):
        for blk in range(TM_IN // TK):
            for hd in range(N_HEADS):
                t_ref[blk, hd] = y[blk * TK:(blk + 1) * TK, hd * V_DIM:(hd + 1) * V_DIM].T.astype(BF16)

    a = proj(0, D_CONV)
    b = proj(D_CONV, D_CONV)
    glu_ref[...] = (a * _sigmoid(b)).astype(BF16)
    zc_ref[...] = _silu(proj(2 * D_CONV, D_CONV)).astype(BF16)
    c0 = 3 * D_CONV
    store_transposed(qt_ref, proj(c0, D_QK) * (LOG2E / math.sqrt(HEAD_DIM)))
    k_ref[...] = proj(c0 + D_QK, D_QK).astype(BF16)
    store_transposed(vt_ref, proj(c0 + 2 * D_QK, D_ATTN))
    za_ref[...] = _silu(proj(c0 + 2 * D_QK + D_ATTN, D_ATTN)).astype(BF16)
    c0 = c0 + 2 * D_QK + 2 * D_ATTN
    for j in range(2 * D_MODEL // 512):
        g_ref[:, j * 512:(j + 1) * 512] = _sigmoid(proj(c0 + j * 512, 512)).astype(BF16)


def _inproj(x2, gpre, w_in):
    T = x2.shape[0]
    d_in = w_in.shape[1]
    row = lambda i: (i, 0)
    fixed = lambda i: (0, 0)
    tr_shape = (T // TK, N_HEADS, V_DIM, TK)
    tr_spec = pl.BlockSpec((TM_IN // TK, N_HEADS, V_DIM, TK), lambda i: (i, 0, 0, 0))
    rows = lambda w: (pl.BlockSpec((TM_IN, w), row), jax.ShapeDtypeStruct((T, w), BF16))
    tr = (tr_spec, jax.ShapeDtypeStruct(tr_shape, BF16))
    outs = (rows(D_CONV), rows(D_CONV), tr, rows(D_QK), tr, rows(D_ATTN), rows(2 * D_MODEL))
    return pl.pallas_call(
        _inproj_kernel,
        grid=(T // TM_IN,),
        in_specs=[
            pl.BlockSpec((TM_IN, D_MODEL), row),
            pl.BlockSpec((1, D_MODEL), fixed),
            pl.BlockSpec((D_MODEL, d_in), fixed, pipeline_mode=pl.Buffered(1)),
        ],
        out_specs=[o[0] for o in outs],
        out_shape=[o[1] for o in outs],
        compiler_params=pltpu.CompilerParams(
            dimension_semantics=("arbitrary",), vmem_limit_bytes=VMEM_LIMIT_BYTES),
        name="inproj",
    )(x2, gpre, w_in)


def _conv_kernel(glu_ref, halo_ref, zc_ref, w_ref, b_ref, lng_ref, lnb_ref, o_ref, xpad_ref, acc_ref):
    i = pl.program_id(1)
    halo = jnp.where(i == 0, 0.0, halo_ref[0].astype(F32))
    cur = glu_ref[0].astype(F32)
    for c in range(D_CONV // 128):
        xpad_ref[c, 0:CONV_HALO, :] = halo[:, c * 128:(c + 1) * 128]
        xpad_ref[c, CONV_HALO:, :] = cur[:, c * 128:(c + 1) * 128]

    shift = CONV_HALO - (CONV_WIDTH - 1)

    def rows(r, carry):
        r0 = pl.multiple_of(r * CONV_ROWS, CONV_ROWS)
        for c in range(D_CONV // 128):
            acc = jnp.zeros((CONV_ROWS, 128), F32)
            for j in range(CONV_WIDTH):
                xs = xpad_ref[c, pl.ds(r0 + (shift + j), CONV_ROWS), :]
                acc = acc + xs * w_ref[j:j + 1, c * 128:(c + 1) * 128]
            acc_ref[pl.ds(r0, CONV_ROWS), c * 128:(c + 1) * 128] = acc
        return carry

    lax.fori_loop(0, TS_CONV // CONV_ROWS, rows, 0)

    y = acc_ref[...] + b_ref[...]
    mu = jnp.mean(y, axis=-1, keepdims=True)
    yc = y - mu
    var = jnp.mean(yc * yc, axis=-1, keepdims=True)
    yn = yc * lax.rsqrt(var + EPS) * lng_ref[...] + lnb_ref[...]
    o_ref[0] = (_silu(yn) * zc_ref[0].astype(F32)).astype(BF16)


def _conv(glu, zc, conv_w, conv_b, ln_g, ln_b):
    B, S, _ = glu.shape
    blocks_per_halo = TS_CONV // CONV_HALO
    cur = lambda b, i: (b, i, 0)
    prev = lambda b, i: (b, jnp.maximum(i * blocks_per_halo - 1, 0), 0)
    fixed = lambda b, i: (0, 0)
    return pl.pallas_call(
        _conv_kernel,
        grid=(B, S // TS_CONV),
        in_specs=[
            pl.BlockSpec((1, TS_CONV, D_CONV), cur),
            pl.BlockSpec((1, CONV_HALO, D_CONV), prev),
            pl.BlockSpec((1, TS_CONV, D_CONV), cur),
            pl.BlockSpec((CONV_WIDTH, D_CONV), fixed),
            pl.BlockSpec((1, D_CONV), fixed),
            pl.BlockSpec((1, D_CONV), fixed),
            pl.BlockSpec((1, D_CONV), fixed),
        ],
        out_specs=pl.BlockSpec((1, TS_CONV, D_CONV), cur),
        out_shape=jax.ShapeDtypeStruct((B, S, D_CONV), BF16),
        scratch_shapes=[
            pltpu.VMEM((D_CONV // 128, TS_CONV + CONV_HALO, 128), F32),
            pltpu.VMEM((TS_CONV, D_CONV), F32),
        ],
        compiler_params=pltpu.CompilerParams(
            dimension_semantics=("arbitrary", "arbitrary"), vmem_limit_bytes=VMEM_LIMIT_BYTES),
        name="conv",
    )(glu, glu, zc, conv_w, conv_b, ln_g, ln_b)


def _bf16_round(x):
    bits = struct.unpack("<I", struct.pack("<f", x))[0]
    bits = (bits + 0x7FFF + ((bits >> 16) & 1)) & 0xFFFF0000
    return struct.unpack("<f", struct.pack("<I", bits))[0]


def _bf16_split3(c):
    c1 = _bf16_round(c)
    c2 = _bf16_round(c - c1)
    c3 = _bf16_round(c - c1 - c2)
    return c1, c2, c3


def _attn_kernel(qt_ref, k_ref, vt_ref, za_ref, lq1_ref, lk1_ref, lq2_ref, lk2_ref, subg_ref,
                 o_ref, qaug_ref, e_ref, tile_ref, acc_ref, m_ref, s_ref, smax_ref, p_ref):
    step = pl.program_id(1)
    slopes2 = [s * LOG2E for s in SLOPES]
    n_blocks = k_ref.shape[1] // TK

    @pl.when((pl.program_id(0) == 0) & (step == 0))
    def _():
        sl = lax.broadcasted_iota(jnp.int32, (TK, 2 * TQ), 0)
        tl = lax.broadcasted_iota(jnp.int32, (TK, 2 * TQ), 1) % TQ
        allowed = (sl // CHUNK) <= (tl // CHUNK)
        rel = (tl - jnp.abs(tl - sl) - sl).astype(F32)
        crow = lax.broadcasted_iota(jnp.int32, (2 * HEAD_DIM, 2 * TQ), 0)
        for h in range(N_HEADS):
            tile_ref[h] = jnp.where(allowed, slopes2[h] * rel, -jnp.inf)
            c1, c2, c3 = _bf16_split3(slopes2[h])
            consts = jnp.where(crow == 0, c1, jnp.where(crow == 1, c2, jnp.where(crow == 2, c3, 0.0)))
            qaug_ref[h, 2 * HEAD_DIM:, :] = consts.astype(BF16)
        lane = lax.broadcasted_iota(jnp.int32, (TK, 2 * HEAD_DIM), 1)
        row = lax.broadcasted_iota(jnp.int32, (TK, 2 * HEAD_DIM), 0)
        e_ref[...] = jnp.where(lane < 3, row, 0).astype(F32).astype(BF16)

    zeros = jnp.zeros((HEAD_DIM, TQ), BF16)
    ones = jnp.ones((16, TK), BF16)
    ones2 = jnp.ones((16, 2 * TK), BF16)

    def colmax(s):
        return jnp.max(s, axis=0, keepdims=True)

    def scores(h, j):
        j0 = pl.multiple_of(j * TK, TK)
        lhs = jnp.concatenate([k_ref[0, pl.ds(j0, TK), h * V_DIM:(h + 1) * V_DIM], e_ref[...]], axis=1)
        return jnp.dot(lhs, qaug_ref[h], preferred_element_type=F32)

    def first_stage(sub, i):
        for h in range(N_HEADS):
            qt = qt_ref[sub, h]
            qaug_ref[h, 0:2 * HEAD_DIM, :] = jnp.concatenate(
                [jnp.concatenate([qt[:HEAD_DIM], zeros], axis=0),
                 jnp.concatenate([zeros, qt[HEAD_DIM:]], axis=0)], axis=1)
        for h in range(N_HEADS):
            s_cur = scores(h, i) + tile_ref[h]
            m_cur = colmax(s_cur)
            s_a = scores(h, 0)
            s_b = scores(h, 1)
            p_ref[h] = jnp.exp2(s_cur - m_cur).astype(BF16)
            m_ref[h] = m_cur
            s_ref[0, 0, h] = s_a
            smax_ref[0, 0, h] = colmax(s_a)
            s_ref[0, 1, h] = s_b
            smax_ref[0, 1, h] = colmax(s_b)
        for h in range(N_HEADS):
            lhs_v = jnp.concatenate([vt_ref[i, h], ones], axis=0)
            acc_ref[h] = jnp.dot(lhs_v, p_ref[h], preferred_element_type=F32)

    def earlier_blocks(i):
        t0 = i * TQ
        n_pairs = i // 2
        odd_pairs = n_pairs % 2

        def shift_of(h, j):
            return slopes2[h] * jnp.full((1, 2 * TQ), j * TK - t0, jnp.int32).astype(F32)

        def update_pair(h, ja, slot):
            sh_a, sh_b = shift_of(h, ja), shift_of(h, ja + 1)
            m_old = m_ref[h]
            m_new = jnp.maximum(m_old, jnp.maximum(smax_ref[slot, 0, h] + sh_a, smax_ref[slot, 1, h] + sh_b))
            p = jnp.concatenate([jnp.exp2(s_ref[slot, 0, h] - (m_new - sh_a)).astype(BF16),
                                 jnp.exp2(s_ref[slot, 1, h] - (m_new - sh_b)).astype(BF16)], axis=0)
            m_ref[h] = m_new
            lhs_v = jnp.concatenate([jnp.concatenate([vt_ref[ja, h], vt_ref[ja + 1, h]], axis=1), ones2], axis=0)
            pv = jnp.dot(lhs_v, p, preferred_element_type=F32)
            acc_ref[h] = acc_ref[h] * jnp.exp2(m_old - m_new) + pv

        def update_single(h, j, slot):
            sh = shift_of(h, j)
            m_old = m_ref[h]
            m_new = jnp.maximum(m_old, smax_ref[slot, 0, h] + sh)
            p = jnp.exp2(s_ref[slot, 0, h] - (m_new - sh)).astype(BF16)
            m_ref[h] = m_new
            lhs_v = jnp.concatenate([vt_ref[j, h], ones], axis=0)
            pv = jnp.dot(lhs_v, p, preferred_element_type=F32)
            acc_ref[h] = acc_ref[h] * jnp.exp2(m_old - m_new) + pv

        def pair_step(jp, rd, wr):
            ja = 2 * jp
            na = jnp.minimum(ja + 2, n_blocks - 1)
            nb = jnp.minimum(ja + 3, n_blocks - 1)
            for h in range(N_HEADS):
                s_a = scores(h, na)
                s_ref[wr, 0, h] = s_a
                smax_ref[wr, 0, h] = colmax(s_a)
                s_b = scores(h, nb)
                s_ref[wr, 1, h] = s_b
                smax_ref[wr, 1, h] = colmax(s_b)
                update_pair(h, ja, rd)

        def two_pairs(jq, carry):
            pair_step(2 * jq, 0, 1)
            pair_step(2 * jq + 1, 1, 0)
            return carry

        def last_pair(_, carry):
            pair_step(n_pairs - 1, 0, 1)
            return carry

        def single(slot):
            def body(_, carry):
                for h in range(N_HEADS):
                    update_single(h, i - 1, slot)
                return carry
            return body

        lax.fori_loop(0, n_pairs // 2, two_pairs, 0)
        lax.fori_loop(0, odd_pairs, last_pair, 0)
        lax.fori_loop(0, (i % 2) * (1 - odd_pairs), single(0), 0)
        lax.fori_loop(0, (i % 2) * odd_pairs, single(1), 0)

    def last_stage(sub):
        lam = (jnp.exp(jnp.sum(lq1_ref[...] * lk1_ref[...], axis=-1, keepdims=True))
               - jnp.exp(jnp.sum(lq2_ref[...] * lk2_ref[...], axis=-1, keepdims=True))
               + LAMBDA_INIT)
        rows = slice(sub * TQ, (sub + 1) * TQ)
        for h in range(N_HEADS):
            hs = slice(h * V_DIM, (h + 1) * V_DIM)
            a = acc_ref[h]
            r = 1.0 / a[V_DIM:V_DIM + 1, :]
            ot = a[:V_DIM, :TQ] * r[:, :TQ] - lam * (a[:V_DIM, TQ:] * r[:, TQ:])
            ot = ot * lax.rsqrt(jnp.mean(ot * ot, axis=0, keepdims=True) + EPS)
            o = ot.T * (subg_ref[...] * (1.0 - LAMBDA_INIT))
            o_ref[0, rows, hs] = (o * za_ref[0, rows, hs].astype(F32)).astype(BF16)

    first_stage(0, step * QB_PER_STEP)
    for sub in range(QB_PER_STEP):
        i = step * QB_PER_STEP + sub
        earlier_blocks(i)
        last_stage(sub)
        if sub + 1 < QB_PER_STEP:
            first_stage(sub + 1, i + 1)


def _attn(qt, k, vt, za, lq1, lk1, lq2, lk2, subg):
    B, S, _ = k.shape
    nq = S // (QB_PER_STEP * TQ)
    blk = lambda b, i: (b, i, 0)
    whole = lambda b, i: (b, 0, 0)
    fixed = lambda b, i: (0, 0)
    return pl.pallas_call(
        _attn_kernel,
        grid=(B, nq),
        in_specs=[
            pl.BlockSpec((QB_PER_STEP, N_HEADS, 2 * HEAD_DIM, TQ), lambda b, i: (b * nq + i, 0, 0, 0)),
            pl.BlockSpec((1, S, D_QK), whole),
            pl.BlockSpec((S // TK, N_HEADS, V_DIM, TK), lambda b, i: (b, 0, 0, 0)),
            pl.BlockSpec((1, QB_PER_STEP * TQ, D_ATTN), blk),
            pl.BlockSpec((1, HEAD_DIM), fixed),
            pl.BlockSpec((1, HEAD_DIM), fixed),
            pl.BlockSpec((1, HEAD_DIM), fixed),
            pl.BlockSpec((1, HEAD_DIM), fixed),
            pl.BlockSpec((1, V_DIM), fixed),
        ],
        out_specs=pl.BlockSpec((1, QB_PER_STEP * TQ, D_ATTN), blk),
        out_shape=jax.ShapeDtypeStruct((B, S, D_ATTN), BF16),
        scratch_shapes=[
            pltpu.VMEM((N_HEADS, 4 * HEAD_DIM, 2 * TQ), BF16),
            pltpu.VMEM((TK, 2 * HEAD_DIM), BF16),
            pltpu.VMEM((N_HEADS, TK, 2 * TQ), F32),
            pltpu.VMEM((N_HEADS, V_DIM + 16, 2 * TQ), F32),
            pltpu.VMEM((N_HEADS, 1, 2 * TQ), F32),
            pltpu.VMEM((2, 2, N_HEADS, TK, 2 * TQ), F32),
            pltpu.VMEM((2, 2, N_HEADS, 1, 2 * TQ), F32),
            pltpu.VMEM((N_HEADS, TK, 2 * TQ), BF16),
        ],
        compiler_params=pltpu.CompilerParams(
            dimension_semantics=("arbitrary", "arbitrary"), vmem_limit_bytes=VMEM_LIMIT_BYTES),
        name="attn",
    )(qt, k, vt, za, lq1, lk1, lq2, lk2, subg)


def _out_kernel(x_ref, hc_ref, ha_ref, g_ref, wc_ref, wa_ref, wo_ref, gpost_ref, o_ref):
    for r in range(TM_OUT // OUT_ROWS):
        rs = slice(r * OUT_ROWS, (r + 1) * OUT_ROWS)
        yc = jnp.dot(hc_ref[rs, :], wc_ref[...], preferred_element_type=F32)
        ya = jnp.dot(ha_ref[rs, :], wa_ref[...], preferred_element_type=F32)
        mixed = (g_ref[rs, :D_MODEL].astype(F32) * yc + g_ref[rs, D_MODEL:].astype(F32) * ya).astype(BF16)
        out = jnp.dot(mixed, wo_ref[...], preferred_element_type=F32)
        ms = jnp.mean(out * out, axis=-1, keepdims=True)
        o_ref[rs, :] = x_ref[rs, :] + out * lax.rsqrt(ms + EPS) * gpost_ref[...]


def _out(x2, hc, ha, g, wc, wa, wo, gpost):
    T = x2.shape[0]
    row = lambda i: (i, 0)
    fixed = lambda i: (0, 0)
    return pl.pallas_call(
        _out_kernel,
        grid=(T // TM_OUT,),
        in_specs=[
            pl.BlockSpec((TM_OUT, D_MODEL), row),
            pl.BlockSpec((TM_OUT, D_CONV), row),
            pl.BlockSpec((TM_OUT, D_ATTN), row),
            pl.BlockSpec((TM_OUT, 2 * D_MODEL), row),
            pl.BlockSpec((D_CONV, D_MODEL), fixed),
            pl.BlockSpec((D_ATTN, D_MODEL), fixed),
            pl.BlockSpec((D_MODEL, D_MODEL), fixed),
            pl.BlockSpec((1, D_MODEL), fixed),
        ],
        out_specs=pl.BlockSpec((TM_OUT, D_MODEL), row),
        out_shape=jax.ShapeDtypeStruct((T, D_MODEL), F32),
        compiler_params=pltpu.CompilerParams(
            dimension_semantics=("arbitrary",), vmem_limit_bytes=VMEM_LIMIT_BYTES),
        name="out",
    )(x2, hc, ha, g, wc, wa, wo, gpost)


def kernel(x, w_in, conv_w, conv_b, conv_ln_g, conv_ln_b, w_conv_proj, lambda_q1, lambda_k1,
           lambda_q2, lambda_k2, subln_g, w_attn_proj, w_out, norm_pre_g, norm_post_g):
    B, S, D = x.shape
    assert (D, w_in.shape[0]) == (D_MODEL, 1) and TQ == TK
    assert S % max(QB_PER_STEP * TQ, TS_CONV) == 0 and (B * S) % TM_IN == 0 and S // TK >= 2
    T = B * S
    x2 = x.reshape(T, D)
    glu, zc, qt, k, vt, za, g = _inproj(x2, norm_pre_g, w_in[0].astype(BF16))
    seq = lambda a: a.reshape(B, S, a.shape[-1])
    hc = _conv(seq(glu), seq(zc), conv_w[0], conv_b, conv_ln_g, conv_ln_b)
    ha = _attn(qt, seq(k), vt, seq(za), lambda_q1, lambda_k1, lambda_q2, lambda_k2, subln_g)
    y = _out(x2, hc.reshape(T, D_CONV), ha.reshape(T, D_ATTN), g,
             w_conv_proj[0].astype(BF16), w_attn_proj[0].astype(BF16), w_out[0].astype(BF16),
             norm_post_g)
    return y.reshape(B, S, D)
```

```python
import math
import struct

import jax
import jax.numpy as jnp
from jax import lax
from jax.experimental import pallas as pl
from jax.experimental.pallas import tpu as pltpu

D_MODEL = 1024
CHUNK = 64
CONV_WIDTH = 31
D_CONV = 512
N_HEADS = 4
HEAD_DIM = 64
V_DIM = 128
D_ATTN = 512
D_QK = 512
EPS = 1e-6
LAMBDA_INIT = 0.8 - 0.6 * math.exp(-0.3 * 0)
LOG2E = math.log2(math.e)
SLOPES = tuple(2.0 ** (-8.0 * (h + 1) / N_HEADS) for h in range(N_HEADS))

BF16 = jnp.bfloat16
F32 = jnp.float32

VMEM_LIMIT_BYTES = 56 * 1024 * 1024

TM_IN = 1024
TS_CONV = 1024
CONV_HALO = 32
CONV_ROWS = 256
TQ = 256
TK = 256
QB_PER_STEP = 2
TM_OUT = 1024


def _sigmoid(x):
    return 1.0 / (1.0 + jnp.exp(-x))


def _silu(x):
    return x * _sigmoid(x)


def _inproj_kernel(x_ref, gpre_ref, w_ref, glu_ref, zc_ref, qt_ref, k_ref, vt_ref, za_ref, g_ref):
    x = x_ref[...]
    ms = jnp.mean(x * x, axis=-1, keepdims=True)
    h = (x * lax.rsqrt(ms + EPS) * gpre_ref[...]).astype(BF16)

    def proj(c0, width):
        return jnp.dot(h, w_ref[:, c0:c0 + width], preferred_element_type=F32)

    def store_transposed(t_ref, y):
        for blk in range(TM_IN // TK):
            for hd in range(N_HEADS):
                t_ref[blk, hd] = y[blk * TK:(blk + 1) * TK, hd * V_DIM:(hd + 1) * V_DIM].T.astype(BF16)

    a = proj(0, D_CONV)
    b = proj(D_CONV, D_CONV)
    glu_ref[...] = (a * _sigmoid(b)).astype(BF16)
    zc_ref[...] = _silu(proj(2 * D_CONV, D_CONV)).astype(BF16)
    c0 = 3 * D_CONV
    store_transposed(qt_ref, proj(c0, D_QK) * (LOG2E / math.sqrt(HEAD_DIM)))
    k_ref[...] = proj(c0 + D_QK, D_QK).astype(BF16)
    store_transposed(vt_ref, proj(c0 + 2 * D_QK, D_ATTN))
    za_ref[...] = _silu(proj(c0 + 2 * D_QK + D_ATTN, D_ATTN)).astype(BF16)
    c0 = c0 + 2 * D_QK + 2 * D_ATTN
    for j in range(2 * D_MODEL // 512):
        g_ref[:, j * 512:(j + 1) * 512] = _sigmoid(proj(c0 + j * 512, 512)).astype(BF16)


def _inproj(x2, gpre, w_in):
    T = x2.shape[0]
    d_in = w_in.shape[1]
    row = lambda i: (i, 0)
    fixed = lambda i: (0, 0)
    tr_shape = (T // TK, N_HEADS, V_DIM, TK)
    tr_spec = pl.BlockSpec((TM_IN // TK, N_HEADS, V_DIM, TK), lambda i: (i, 0, 0, 0))
    rows = lambda w: (pl.BlockSpec((TM_IN, w), row), jax.ShapeDtypeStruct((T, w), BF16))
    tr = (tr_spec, jax.ShapeDtypeStruct(tr_shape, BF16))
    outs = (rows(D_CONV), rows(D_CONV), tr, rows(D_QK), tr, rows(D_ATTN), rows(2 * D_MODEL))
    return pl.pallas_call(
        _inproj_kernel,
        grid=(T // TM_IN,),
        in_specs=[
            pl.BlockSpec((TM_IN, D_MODEL), row),
            pl.BlockSpec((1, D_MODEL), fixed),
            pl.BlockSpec((D_MODEL, d_in), fixed, pipeline_mode=pl.Buffered(1)),
        ],
        out_specs=[o[0] for o in outs],
        out_shape=[o[1] for o in outs],
        compiler_params=pltpu.CompilerParams(
            dimension_semantics=("arbitrary",), vmem_limit_bytes=VMEM_LIMIT_BYTES),
        name="inproj",
    )(x2, gpre, w_in)


def _conv_kernel(glu_ref, halo_ref, zc_ref, w_ref, b_ref, lng_ref, lnb_ref, o_ref, xpad_ref, acc_ref):
    i = pl.program_id(1)
    halo = jnp.where(i == 0, 0.0, halo_ref[0].astype(F32))
    cur = glu_ref[0].astype(F32)
    for c in range(D_CONV // 128):
        xpad_ref[c, 0:CONV_HALO, :] = halo[:, c * 128:(c + 1) * 128]
        xpad_ref[c, CONV_HALO:, :] = cur[:, c * 128:(c + 1) * 128]

    shift = CONV_HALO - (CONV_WIDTH - 1)

    def rows(r, carry):
        r0 = pl.multiple_of(r * CONV_ROWS, CONV_ROWS)
        for c in range(D_CONV // 128):
            acc = jnp.zeros((CONV_ROWS, 128), F32)
            for j in range(CONV_WIDTH):
                xs = xpad_ref[c, pl.ds(r0 + (shift + j), CONV_ROWS), :]
                acc = acc + xs * w_ref[j:j + 1, c * 128:(c + 1) * 128]
            acc_ref[pl.ds(r0, CONV_ROWS), c * 128:(c + 1) * 128] = acc
        return carry

    lax.fori_loop(0, TS_CONV // CONV_ROWS, rows, 0)

    y = acc_ref[...] + b_ref[...]
    mu = jnp.mean(y, axis=-1, keepdims=True)
    yc = y - mu
    var = jnp.mean(yc * yc, axis=-1, keepdims=True)
    yn = yc * lax.rsqrt(var + EPS) * lng_ref[...] + lnb_ref[...]
    o_ref[0] = (_silu(yn) * zc_ref[0].astype(F32)).astype(BF16)


def _conv(glu, zc, conv_w, conv_b, ln_g, ln_b):
    B, S, _ = glu.shape
    blocks_per_halo = TS_CONV // CONV_HALO
    cur = lambda b, i: (b, i, 0)
    prev = lambda b, i: (b, jnp.maximum(i * blocks_per_halo - 1, 0), 0)
    fixed = lambda b, i: (0, 0)
    return pl.pallas_call(
        _conv_kernel,
        grid=(B, S // TS_CONV),
        in_specs=[
            pl.BlockSpec((1, TS_CONV, D_CONV), cur),
            pl.BlockSpec((1, CONV_HALO, D_CONV), prev),
            pl.BlockSpec((1, TS_CONV, D_CONV), cur),
            pl.BlockSpec((CONV_WIDTH, D_CONV), fixed),
            pl.BlockSpec((1, D_CONV), fixed),
            pl.BlockSpec((1, D_CONV), fixed),
            pl.BlockSpec((1, D_CONV), fixed),
        ],
        out_specs=pl.BlockSpec((1, TS_CONV, D_CONV), cur),
        out_shape=jax.ShapeDtypeStruct((B, S, D_CONV), BF16),
        scratch_shapes=[
            pltpu.VMEM((D_CONV // 128, TS_CONV + CONV_HALO, 128), F32),
            pltpu.VMEM((TS_CONV, D_CONV), F32),
        ],
        compiler_params=pltpu.CompilerParams(
            dimension_semantics=("arbitrary", "arbitrary"), vmem_limit_bytes=VMEM_LIMIT_BYTES),
        name="conv",
    )(glu, glu, zc, conv_w, conv_b, ln_g, ln_b)


def _bf16_round(x):
    bits = struct.unpack("<I", struct.pack("<f", x))[0]
    bits = (bits + 0x7FFF + ((bits >> 16) & 1)) & 0xFFFF0000
    return struct.unpack("<f", struct.pack("<I", bits))[0]


def _bf16_split3(c):
    c1 = _bf16_round(c)
    c2 = _bf16_round(c - c1)
    c3 = _bf16_round(c - c1 - c2)
    return c1, c2, c3


def _attn_kernel(qt_ref, k_ref, vt_ref, za_ref, lq1_ref, lk1_ref, lq2_ref, lk2_ref, subg_ref,
                 o_ref, qaug_ref, e_ref, tile_ref, acc_ref, m_ref, s_ref, smax_ref, p_ref):
    step = pl.program_id(1)
    slopes2 = [s * LOG2E for s in SLOPES]
    n_blocks = k_ref.shape[1] // TK

    @pl.when((pl.program_id(0) == 0) & (step == 0))
    def _():
        sl = lax.broadcasted_iota(jnp.int32, (TK, 2 * TQ), 0)
        tl = lax.broadcasted_iota(jnp.int32, (TK, 2 * TQ), 1) % TQ
        allowed = (sl // CHUNK) <= (tl // CHUNK)
        rel = (tl - jnp.abs(tl - sl) - sl).astype(F32)
        crow = lax.broadcasted_iota(jnp.int32, (2 * HEAD_DIM, 2 * TQ), 0)
        for h in range(N_HEADS):
            tile_ref[h] = jnp.where(allowed, slopes2[h] * rel, -jnp.inf)
            c1, c2, c3 = _bf16_split3(slopes2[h])
            consts = jnp.where(crow == 0, c1, jnp.where(crow == 1, c2, jnp.where(crow == 2, c3, 0.0)))
            qaug_ref[h, 2 * HEAD_DIM:, :] = consts.astype(BF16)
        lane = lax.broadcasted_iota(jnp.int32, (TK, 2 * HEAD_DIM), 1)
        row = lax.broadcasted_iota(jnp.int32, (TK, 2 * HEAD_DIM), 0)
        e_ref[...] = jnp.where(lane < 3, row, 0).astype(F32).astype(BF16)

    zeros = jnp.zeros((HEAD_DIM, TQ), BF16)
    ones = jnp.ones((16, TK), BF16)
    ones2 = jnp.ones((16, 2 * TK), BF16)

    def colmax(s):
        return jnp.max(s, axis=0, keepdims=True)

    def scores(h, j):
        j0 = pl.multiple_of(j * TK, TK)
        lhs = jnp.concatenate([k_ref[0, pl.ds(j0, TK), h * V_DIM:(h + 1) * V_DIM], e_ref[...]], axis=1)
        return jnp.dot(lhs, qaug_ref[h], preferred_element_type=F32)

    def first_stage(sub, i):
        for h in range(N_HEADS):
            qt = qt_ref[sub, h]
            qaug_ref[h, 0:2 * HEAD_DIM, :] = jnp.concatenate(
                [jnp.concatenate([qt[:HEAD_DIM], zeros], axis=0),
                 jnp.concatenate([zeros, qt[HEAD_DIM:]], axis=0)], axis=1)
        for h in range(N_HEADS):
            s_cur = scores(h, i) + tile_ref[h]
            m_cur = colmax(s_cur)
            s_a = scores(h, 0)
            s_b = scores(h, 1)
            p_ref[h] = jnp.exp2(s_cur - m_cur).astype(BF16)
            m_ref[h] = m_cur
            s_ref[0, 0, h] = s_a
            smax_ref[0, 0, h] = colmax(s_a)
            s_ref[0, 1, h] = s_b
            smax_ref[0, 1, h] = colmax(s_b)
        for h in range(N_HEADS):
            lhs_v = jnp.concatenate([vt_ref[i, h], ones], axis=0)
            acc_ref[h] = jnp.dot(lhs_v, p_ref[h], preferred_element_type=F32)

    def earlier_blocks(i):
        t0 = i * TQ
        n_pairs = i // 2
        odd_pairs = n_pairs % 2

        def shift_of(h, j):
            return slopes2[h] * jnp.full((1, 2 * TQ), j * TK - t0, jnp.int32).astype(F32)

        def update_pair(h, ja, slot):
            sh_a, sh_b = shift_of(h, ja), shift_of(h, ja + 1)
            m_old = m_ref[h]
            m_new = jnp.maximum(m_old, jnp.maximum(smax_ref[slot, 0, h] + sh_a, smax_ref[slot, 1, h] + sh_b))
            p = jnp.concatenate([jnp.exp2(s_ref[slot, 0, h] - (m_new - sh_a)).astype(BF16),
                                 jnp.exp2(s_ref[slot, 1, h] - (m_new - sh_b)).astype(BF16)], axis=0)
            m_ref[h] = m_new
            lhs_v = jnp.concatenate([jnp.concatenate([vt_ref[ja, h], vt_ref[ja + 1, h]], axis=1), ones2], axis=0)
            pv = jnp.dot(lhs_v, p, preferred_element_type=F32)
            acc_ref[h] = acc_ref[h] * jnp.exp2(m_old - m_new) + pv

        def update_single(h, j, slot):
            sh = shift_of(h, j)
            m_old = m_ref[h]
            m_new = jnp.maximum(m_old, smax_ref[slot, 0, h] + sh)
            p = jnp.exp2(s_ref[slot, 0, h] - (m_new - sh)).astype(BF16)
            m_ref[h] = m_new
            lhs_v = jnp.concatenate([vt_ref[j, h], ones], axis=0)
            pv = jnp.dot(lhs_v, p, preferred_element_type=F32)
            acc_ref[h] = acc_ref[h] * jnp.exp2(m_old - m_new) + pv

        def pair_step(jp, rd, wr):
            ja = 2 * jp
            na = jnp.minimum(ja + 2, n_blocks - 1)
            nb = jnp.minimum(ja + 3, n_blocks - 1)
            for h in range(N_HEADS):
                s_a = scores(h, na)
                s_ref[wr, 0, h] = s_a
                smax_ref[wr, 0, h] = colmax(s_a)
                s_b = scores(h, nb)
                s_ref[wr, 1, h] = s_b
                smax_ref[wr, 1, h] = colmax(s_b)
                update_pair(h, ja, rd)

        def four_pairs(jq, carry):
            pair_step(4 * jq, 0, 1)
            pair_step(4 * jq + 1, 1, 0)
            pair_step(4 * jq + 2, 0, 1)
            pair_step(4 * jq + 3, 1, 0)
            return carry

        def two_pairs(_, carry):
            pair_step(n_pairs - 2 - odd_pairs, 0, 1)
            pair_step(n_pairs - 1 - odd_pairs, 1, 0)
            return carry

        def last_pair(_, carry):
            pair_step(n_pairs - 1, 0, 1)
            return carry

        def single(slot):
            def body(_, carry):
                for h in range(N_HEADS):
                    update_single(h, i - 1, slot)
                return carry
            return body

        lax.fori_loop(0, n_pairs // 4, four_pairs, 0)
        lax.fori_loop(0, (n_pairs // 2) % 2, two_pairs, 0)
        lax.fori_loop(0, odd_pairs, last_pair, 0)
        lax.fori_loop(0, (i % 2) * (1 - odd_pairs), single(0), 0)
        lax.fori_loop(0, (i % 2) * odd_pairs, single(1), 0)

    def last_stage(sub):
        lam = (jnp.exp(jnp.sum(lq1_ref[...] * lk1_ref[...], axis=-1, keepdims=True))
               - jnp.exp(jnp.sum(lq2_ref[...] * lk2_ref[...], axis=-1, keepdims=True))
               + LAMBDA_INIT)
        rows = slice(sub * TQ, (sub + 1) * TQ)
        for h in range(N_HEADS):
            hs = slice(h * V_DIM, (h + 1) * V_DIM)
            a = acc_ref[h]
            r = 1.0 / a[V_DIM:V_DIM + 1, :]
            ot = a[:V_DIM, :TQ] * r[:, :TQ] - lam * (a[:V_DIM, TQ:] * r[:, TQ:])
            ot = ot * lax.rsqrt(jnp.mean(ot * ot, axis=0, keepdims=True) + EPS)
            o = ot.T * (subg_ref[...] * (1.0 - LAMBDA_INIT))
            o_ref[0, rows, hs] = (o * za_ref[0, rows, hs].astype(F32)).astype(BF16)

    first_stage(0, step * QB_PER_STEP)
    for sub in range(QB_PER_STEP):
        i = step * QB_PER_STEP + sub
        earlier_blocks(i)
        last_stage(sub)
        if sub + 1 < QB_PER_STEP:
            first_stage(sub + 1, i + 1)


def _attn(qt, k, vt, za, lq1, lk1, lq2, lk2, subg):
    B, S, _ = k.shape
    nq = S // (QB_PER_STEP * TQ)
    blk = lambda b, i: (b, i, 0)
    whole = lambda b, i: (b, 0, 0)
    fixed = lambda b, i: (0, 0)
    return pl.pallas_call(
        _attn_kernel,
        grid=(B, nq),
        in_specs=[
            pl.BlockSpec((QB_PER_STEP, N_HEADS, 2 * HEAD_DIM, TQ), lambda b, i: (b * nq + i, 0, 0, 0)),
            pl.BlockSpec((1, S, D_QK), whole),
            pl.BlockSpec((S // TK, N_HEADS, V_DIM, TK), lambda b, i: (b, 0, 0, 0)),
            pl.BlockSpec((1, QB_PER_STEP * TQ, D_ATTN), blk),
            pl.BlockSpec((1, HEAD_DIM), fixed),
            pl.BlockSpec((1, HEAD_DIM), fixed),
            pl.BlockSpec((1, HEAD_DIM), fixed),
            pl.BlockSpec((1, HEAD_DIM), fixed),
            pl.BlockSpec((1, V_DIM), fixed),
        ],
        out_specs=pl.BlockSpec((1, QB_PER_STEP * TQ, D_ATTN), blk),
        out_shape=jax.ShapeDtypeStruct((B, S, D_ATTN), BF16),
        scratch_shapes=[
            pltpu.VMEM((N_HEADS, 4 * HEAD_DIM, 2 * TQ), BF16),
            pltpu.VMEM((TK, 2 * HEAD_DIM), BF16),
            pltpu.VMEM((N_HEADS, TK, 2 * TQ), F32),
            pltpu.VMEM((N_HEADS, V_DIM + 16, 2 * TQ), F32),
            pltpu.VMEM((N_HEADS, 1, 2 * TQ), F32),
            pltpu.VMEM((2, 2, N_HEADS, TK, 2 * TQ), F32),
            pltpu.VMEM((2, 2, N_HEADS, 1, 2 * TQ), F32),
            pltpu.VMEM((N_HEADS, TK, 2 * TQ), BF16),
        ],
        compiler_params=pltpu.CompilerParams(
            dimension_semantics=("arbitrary", "arbitrary"), vmem_limit_bytes=VMEM_LIMIT_BYTES),
        name="attn",
    )(qt, k, vt, za, lq1, lk1, lq2, lk2, subg)


def _out_kernel(x_ref, hc_ref, ha_ref, g_ref, wc_ref, wa_ref, wo_ref, gpost_ref, o_ref):
    yc = jnp.dot(hc_ref[...], wc_ref[...], preferred_element_type=F32)
    ya = jnp.dot(ha_ref[...], wa_ref[...], preferred_element_type=F32)
    mixed = (g_ref[:, :D_MODEL].astype(F32) * yc + g_ref[:, D_MODEL:].astype(F32) * ya).astype(BF16)
    out = jnp.dot(mixed, wo_ref[...], preferred_element_type=F32)
    ms = jnp.mean(out * out, axis=-1, keepdims=True)
    o_ref[...] = x_ref[...] + out * lax.rsqrt(ms + EPS) * gpost_ref[...]


def _out(x2, hc, ha, g, wc, wa, wo, gpost):
    T = x2.shape[0]
    row = lambda i: (i, 0)
    fixed = lambda i: (0, 0)
    return pl.pallas_call(
        _out_kernel,
        grid=(T // TM_OUT,),
        in_specs=[
            pl.BlockSpec((TM_OUT, D_MODEL), row),
            pl.BlockSpec((TM_OUT, D_CONV), row),
            pl.BlockSpec((TM_OUT, D_ATTN), row),
            pl.BlockSpec((TM_OUT, 2 * D_MODEL), row),
            pl.BlockSpec((D_CONV, D_MODEL), fixed),
            pl.BlockSpec((D_ATTN, D_MODEL), fixed),
            pl.BlockSpec((D_MODEL, D_MODEL), fixed),
            pl.BlockSpec((1, D_MODEL), fixed),
        ],
        out_specs=pl.BlockSpec((TM_OUT, D_MODEL), row),
        out_shape=jax.ShapeDtypeStruct((T, D_MODEL), F32),
        compiler_params=pltpu.CompilerParams(
            dimension_semantics=("arbitrary",), vmem_limit_bytes=VMEM_LIMIT_BYTES),
        name="out",
    )(x2, hc, ha, g, wc, wa, wo, gpost)


def kernel(x, w_in, conv_w, conv_b, conv_ln_g, conv_ln_b, w_conv_proj, lambda_q1, lambda_k1,
           lambda_q2, lambda_k2, subln_g, w_attn_proj, w_out, norm_pre_g, norm_post_g):
    B, S, D = x.shape
    assert (D, w_in.shape[0]) == (D_MODEL, 1) and TQ == TK
    assert S % max(QB_PER_STEP * TQ, TS_CONV) == 0 and (B * S) % TM_IN == 0 and S // TK >= 2
    T = B * S
    x2 = x.reshape(T, D)
    glu, zc, qt, k, vt, za, g = _inproj(x2, norm_pre_g, w_in[0].astype(BF16))
    seq = lambda a: a.reshape(B, S, a.shape[-1])
    hc = _conv(seq(glu), seq(zc), conv_w[0], conv_b, conv_ln_g, conv_ln_b)
    ha = _attn(qt, seq(k), vt, seq(za), lambda_q1, lambda_k1, lambda_q2, lambda_k2, subln_g)
    y = _out(x2, hc.reshape(T, D_CONV), ha.reshape(T, D_ATTN), g,
             w_conv_proj[0].astype(BF16), w_attn_proj[0].astype(BF16), w_out[0].astype(BF16),
             norm_post_g)
    return y.reshape(B, S, D)
```
